```python
import math
import jax
import jax.numpy as jnp
from jax import lax
import numpy as np

D_MODEL = 2048
BATCH = 16
SEQ = 256
DEPTH = 4
DEC_BATCH = 4
DEC_SEQ = 4096
PAST_LEN = 256

GRID_W = 64
HA = 8
DA = 64
DVA = 2 * DA
HH = 8
HDK = 128
HDV = 128
HG = 4
GDK = 128
GDV = 256
GLA_RANK = 16
GLA_TAU = 16.0
BRANCH_W = 1024
N_BRANCH = 3
FFN_DIM = 5632
ROPE_AX = DA // 2
ROPE_BASE = 10000.0
Q_BLOCK = 128
CHUNK = 32
EPS = 1e-6
IN_SIZES = (HA * 2 * DA, HA * 2 * DA, HA * DVA,
            HH * HDK, HH * HDK, HH * HDK, HH * HDV, HH * HDV,
            HG * GDK, HG * GDK, HG * GDV, HG * GDV, 2 * GLA_RANK,
            N_BRANCH * D_MODEL)
N_IN = sum(IN_SIZES)

kernel_name = 'hybrid_diffattn_hgrn2_gla_dit_step'


def rms_norm(x, g):
    xf = x.astype(jnp.float32)
    y = xf * lax.rsqrt(jnp.mean(xf * xf, axis=-1, keepdims=True) + EPS)
    return (y * g.astype(jnp.float32)).astype(x.dtype)


def modulate(x, g, shift, scale):
    return rms_norm(x, g) * (1 + scale) + shift


def swiglu(h, w_gate, w_up, w_down):
    return (jax.nn.silu(h @ w_gate) * (h @ w_up)) @ w_down


def axial_rope(n_tok):
    rows = n_tok // GRID_W
    r = jnp.broadcast_to(jnp.arange(rows, dtype=jnp.float32)[:, None], (rows, GRID_W)).reshape(-1)
    col = jnp.broadcast_to(jnp.arange(GRID_W, dtype=jnp.float32)[None, :], (rows, GRID_W)).reshape(-1)
    inv = ROPE_BASE ** (-jnp.arange(0, ROPE_AX, 2, dtype=jnp.float32) / ROPE_AX)
    ar, ac = r[:, None] * inv, col[:, None] * inv
    ang = jnp.concatenate([ar, ar, ac, ac], axis=-1)
    return jnp.cos(ang), jnp.sin(ang)


def rotate_axial(x):
    xr = x.reshape(x.shape[:-1] + (2, 2, ROPE_AX // 2))
    return jnp.concatenate([-xr[..., 1:, :], xr[..., :1, :]], axis=-2).reshape(x.shape)


def apply_axial_rope(x, cos, sin):
    xf = x.astype(jnp.float32)
    return (xf * cos[:, None] + rotate_axial(xf) * sin[:, None]).astype(x.dtype)


def diff_attention(q, k, v, lam):
    B, H, N = q.shape[:3]
    nb = N // Q_BLOCK
    qb = jnp.moveaxis(q.reshape(B, H, nb, Q_BLOCK, 2, DA), 2, 0)

    def block(qblk):
        s = jnp.einsum('bhqcd,bhkcd->bhcqk', qblk, k).astype(jnp.float32) * (DA ** -0.5)
        p = jax.nn.softmax(s, axis=-1)
        w = p[:, :, 0] - lam * p[:, :, 1]
        return jnp.einsum('bhqk,bhkd->bhqd', w.astype(v.dtype), v)

    o = lax.map(block, qb)
    return jnp.moveaxis(o, 0, 2).reshape(B, H, N, DVA)


def chunk_gla(q, k, v, log_a, s0):
    dtype = v.dtype
    B, H, T, dk = q.shape
    dv = v.shape[-1]
    n = T // CHUNK

    def chunks(t):
        return jnp.moveaxis(t.astype(jnp.float32).reshape(B, H, n, CHUNK, t.shape[-1]), 2, 0)

    mask = jnp.tril(jnp.ones((CHUNK, CHUNK), dtype=bool))[:, :, None]

    def step(s, blk):
        qc, kc, vc, ac = blk
        b = jnp.cumsum(ac, axis=2)
        rel = jnp.where(mask, b[:, :, :, None, :] - b[:, :, None, :, :], -jnp.inf)
        scores = jnp.einsum('bhid,bhjd,bhijd->bhij', qc, kc, jnp.exp(rel))
        o = jnp.einsum('bhij,bhje->bhie', scores, vc) + jnp.einsum('bhid,bhde->bhie', qc * jnp.exp(b), s)
        b_end = b[:, :, -1:, :]
        s = jnp.exp(b_end[:, :, 0, :, None]) * s + jnp.einsum('bhjd,bhje->bhde', kc * jnp.exp(b_end - b), vc)
        return s, o

    s_T, o = lax.scan(step, s0.astype(jnp.float32), (chunks(q), chunks(k), chunks(v), chunks(log_a)))
    o = jnp.moveaxis(o, 0, 2).reshape(B, H, T, dv)
    return o.astype(dtype), s_T


def bidir_scan(q, k_f, k_b, v, la_f, la_b, s0):
    flip = lambda t: jnp.flip(t, axis=2)
    o_f, s_f = chunk_gla(q, k_f, v, la_f, s0[:, 0])
    o_b, s_b = chunk_gla(flip(q), flip(k_b), flip(v), flip(la_b), s0[:, 1])
    return o_f + flip(o_b), jnp.stack([s_f, s_b], axis=1)


def token_mixers(h, l, P, lb, rope, ctx):
    B, T, _ = h.shape
    split_at = np.cumsum(IN_SIZES)[:-1].tolist()
    (aq, ak, av, hq, hf_f, hf_b, hi, hg, gq, gk, gv, gg, glr, mg) = jnp.split(h @ P['w_in'], split_at, axis=-1)

    def to_heads(t, n):
        return t.reshape(B, T, n, -1).transpose(0, 2, 1, 3)

    def from_heads(t):
        return t.transpose(0, 2, 1, 3).reshape(B, T, -1)

    lam_init = 0.8 - 0.6 * math.exp(-0.3 * l)
    lq1, lk1, lq2, lk2 = P['attn_lambda'].astype(jnp.float32)
    lam = jnp.exp(jnp.sum(lq1 * lk1)) - jnp.exp(jnp.sum(lq2 * lk2)) + lam_init
    q = rms_norm(aq.reshape(B, T, HA, 2, DA).transpose(0, 2, 1, 3, 4), P['attn_qk_norm'][0])
    k = rms_norm(ak.reshape(B, T, HA, 2, DA).transpose(0, 2, 1, 3, 4), P['attn_qk_norm'][1])
    v = to_heads(av, HA)
    if ctx is None:
        keys, vals = k, v
        s0_h = jnp.zeros((B, 2, HH, HDK, HDV), jnp.float32)
        s0_g = jnp.zeros((B, 2, HG, GDK, GDV), jnp.float32)
    else:
        cos, sin = rope
        q = apply_axial_rope(q, cos, sin)
        k = apply_axial_rope(k, cos, sin)
        keys = jnp.concatenate([ctx[0].astype(k.dtype), k], axis=2)
        vals = jnp.concatenate([ctx[1].astype(v.dtype), v], axis=2)
        s0_h, s0_g = ctx[2], ctx[3]
    o = diff_attention(q, keys, vals, lam)
    o_attn = from_heads(rms_norm(o, P['attn_out_norm']) * (1 - lam_init))

    f_fwd = lb[0] + (1 - lb[0]) * jax.nn.sigmoid(hf_f.astype(jnp.float32))
    f_bwd = lb[1] + (1 - lb[1]) * jax.nn.sigmoid(hf_b.astype(jnp.float32))
    o_h, s_h = bidir_scan(to_heads(jax.nn.silu(hq), HH),
                          to_heads(1 - f_fwd, HH), to_heads(1 - f_bwd, HH),
                          to_heads(hi, HH),
                          to_heads(jnp.log(f_fwd), HH), to_heads(jnp.log(f_bwd), HH), s0_h)
    o_hgrn = from_heads(rms_norm(o_h, P['hgrn_out_norm']) * jax.nn.silu(to_heads(hg, HH)))

    decay_logit = jnp.einsum('btzr,zrk->btzk', glr.reshape(B, T, 2, GLA_RANK), P['gla_w_decay']) + P['gla_b_decay']
    la = jax.nn.log_sigmoid(decay_logit.astype(jnp.float32)) / GLA_TAU
    k_g = to_heads(gk, HG)
    o_g, s_g = bidir_scan(to_heads(gq, HG) * (GDK ** -0.5), k_g, k_g, to_heads(gv, HG),
                          to_heads(la[:, :, 0], HG), to_heads(la[:, :, 1], HG), s0_g)
    o_gla = from_heads(rms_norm(o_g, P['gla_out_norm']) * jax.nn.silu(to_heads(gg, HG)))

    br = jnp.stack([o_attn, o_hgrn, o_gla], axis=2)
    up = jnp.einsum('btzc,zcd->btzd', br, P['w_branch'])
    gates = jax.nn.sigmoid(mg.reshape(B, T, N_BRANCH, D_MODEL))
    out = jnp.sum(gates * up, axis=2) @ P['w_out']
    ctx_out = (k, v, s_h, s_g) if ctx is None else None
    return out, ctx_out


def trunk_layer(x, cond, l, P, lb, rope, ctx):
    m = (jax.nn.silu(cond) @ P['ada_w'] + P['ada_b']).reshape(cond.shape[0], 1, 9, D_MODEL)
    h = modulate(x, P['norm_w'][0], m[:, :, 0], m[:, :, 1])
    x = x + 0.5 * m[:, :, 2] * swiglu(h, P['ffn_w_gate'][0], P['ffn_w_up'][0], P['ffn_w_down'][0])
    h = modulate(x, P['norm_w'][1], m[:, :, 3], m[:, :, 4])
    mix, ctx_out = token_mixers(h, l, P, lb, rope, ctx)
    x = x + m[:, :, 5] * mix
    h = modulate(x, P['norm_w'][2], m[:, :, 6], m[:, :, 7])
    x = x + 0.5 * m[:, :, 8] * swiglu(h, P['ffn_w_gate'][1], P['ffn_w_up'][1], P['ffn_w_down'][1])
    return x, ctx_out


def setup_inputs(seed: int = 0) -> dict:
    key = jax.random.key(seed)
    ks = jax.random.split(key, 25)
    D = D_MODEL

    def nrm(k, shape, s):
        return jax.random.normal(k, shape, jnp.float32) * s

    return {
        'x_prompt': nrm(ks[0], (BATCH, SEQ, D), 1.0),
        'x_sample': nrm(ks[1], (DEC_BATCH, DEC_SEQ, D), 1.0),
        'c': nrm(ks[2], (DEC_BATCH, D), 1.0),
        'cache_attn_k': nrm(ks[3], (DEC_BATCH, DEPTH, HA, PAST_LEN, 2, DA), 1.0),
        'cache_attn_v': nrm(ks[4], (DEC_BATCH, DEPTH, HA, PAST_LEN, DVA), 1.0),
        'state_hgrn': nrm(ks[5], (DEC_BATCH, DEPTH, 2, HH, HDK, HDV), 0.5),
        'state_gla': nrm(ks[6], (DEC_BATCH, DEPTH, 2, HG, GDK, GDV), 2.0),
        'c_ctx': nrm(ks[7], (D,), 1.0),
        'ada_w': nrm(ks[8], (DEPTH, D, 9 * D), 0.5 * D ** -0.5),
        'ada_b': nrm(ks[9], (DEPTH, 9 * D), 0.02),
        'norm_w': 1.0 + nrm(ks[10], (DEPTH, 3, D), 0.02),
        'ffn_w_gate': nrm(ks[11], (DEPTH, 2, D, FFN_DIM), D ** -0.5),
        'ffn_w_up': nrm(ks[12], (DEPTH, 2, D, FFN_DIM), D ** -0.5),
        'ffn_w_down': nrm(ks[13], (DEPTH, 2, FFN_DIM, D), FFN_DIM ** -0.5),
        'w_in': nrm(ks[14], (DEPTH, D, N_IN), D ** -0.5),
        'attn_qk_norm': 1.0 + nrm(ks[15], (DEPTH, 2, DA), 0.02),
        'attn_lambda': nrm(ks[16], (DEPTH, 4, DA), 0.1),
        'attn_out_norm': 1.0 + nrm(ks[17], (DEPTH, DVA), 0.02),
        'hgrn_lb_logits': nrm(ks[18], (DEPTH, 2, HH * HDK), 0.5),
        'hgrn_out_norm': 1.0 + nrm(ks[19], (DEPTH, HDV), 0.02),
        'gla_w_decay': nrm(ks[20], (DEPTH, 2, GLA_RANK, HG * GDK), GLA_RANK ** -0.5),
        'gla_b_decay': nrm(ks[21], (DEPTH, 2, HG * GDK), 0.1),
        'gla_out_norm': 1.0 + nrm(ks[22], (DEPTH, GDV), 0.02),
        'w_branch': nrm(ks[23], (DEPTH, N_BRANCH, BRANCH_W, D), BRANCH_W ** -0.5),
        'w_out': nrm(ks[24], (DEPTH, D, D), D ** -0.5),
    }


def reference(x_prompt, x_sample, c, cache_attn_k, cache_attn_v, state_hgrn, state_gla, c_ctx,
              ada_w, ada_b, norm_w, ffn_w_gate, ffn_w_up, ffn_w_down, w_in,
              attn_qk_norm, attn_lambda, attn_out_norm, hgrn_lb_logits, hgrn_out_norm,
              gla_w_decay, gla_b_decay, gla_out_norm, w_branch, w_out):
    lb_w = jax.nn.softmax(hgrn_lb_logits.astype(jnp.float32), axis=0)
    lower_bounds = jnp.cumsum(lb_w, axis=0) - lb_w[0]
    rope = axial_rope(x_sample.shape[1])
    cond_ctx = c_ctx[None]
    y_prompt, y_sample = x_prompt, x_sample
    ks_l, vs_l, sh_l, sg_l = [], [], [], []
    for l in range(DEPTH):
        P = {'ada_w': ada_w[l], 'ada_b': ada_b[l], 'norm_w': norm_w[l],
             'ffn_w_gate': ffn_w_gate[l], 'ffn_w_up': ffn_w_up[l], 'ffn_w_down': ffn_w_down[l],
             'w_in': w_in[l], 'attn_qk_norm': attn_qk_norm[l], 'attn_lambda': attn_lambda[l],
             'attn_out_norm': attn_out_norm[l], 'hgrn_out_norm': hgrn_out_norm[l],
             'gla_w_decay': gla_w_decay[l], 'gla_b_decay': gla_b_decay[l], 'gla_out_norm': gla_out_norm[l],
             'w_branch': w_branch[l], 'w_out': w_out[l]}
        lb = lower_bounds[l]
        y_prompt, (k_c, v_c, s_h, s_g) = trunk_layer(y_prompt, cond_ctx, l, P, lb, None, None)
        ks_l.append(k_c)
        vs_l.append(v_c)
        sh_l.append(s_h.astype(x_prompt.dtype))
        sg_l.append(s_g.astype(x_prompt.dtype))
        y_sample, _ = trunk_layer(y_sample, c, l, P, lb, rope,
                                  (cache_attn_k[:, l], cache_attn_v[:, l], state_hgrn[:, l], state_gla[:, l]))
    new_attn_k = jnp.stack(ks_l, axis=1)
    new_attn_v = jnp.stack(vs_l, axis=1)
    new_state_hgrn = jnp.stack(sh_l, axis=1)
    new_state_gla = jnp.stack(sg_l, axis=1)
    return (y_prompt, y_sample, new_attn_k, new_attn_v, new_state_hgrn, new_state_gla)
```

```python
import functools
import math

import jax
import jax.numpy as jnp
import numpy as np
from jax import lax
from jax.experimental import pallas as pl
from jax.experimental.pallas import tpu as pltpu

F32 = jnp.float32
BF16 = jnp.bfloat16

GRID_W = 64
GLA_TAU = 16.0
ROPE_BASE = 10000.0
EPS = 1e-6
LANES = 128
SCAN_CHUNK = 64
SCAN_BLOCK = 256
SAFE_EXPONENT = 80.0
VMEM_LIMIT = 56 * 1024 * 1024


def _dot(a, b):
    return jnp.dot(a, b, preferred_element_type=F32)


def _dot_nt(a, b):
    return lax.dot_general(a, b, (((1,), (1,)), ((), ())), preferred_element_type=F32)


def _dot_tn(a, b):
    return lax.dot_general(a, b, (((0,), (0,)), ((), ())), preferred_element_type=F32)


def _sigmoid(x):
    return 1.0 / (1.0 + jnp.exp(-x))


def _silu(x):
    return x * _sigmoid(x)


def _split3(x):
    hi = x.astype(BF16)
    r1 = x - hi.astype(F32)
    mid = r1.astype(BF16)
    lo = (r1 - mid.astype(F32)).astype(BF16)
    return hi, mid, lo


def _dot_exact_lhs(m, parts):
    return _dot(m, parts[0]) + _dot(m, parts[1]) + _dot(m, parts[2])


def _pick_tile(n, candidates):
    for c in candidates:
        if n % c == 0:
            return c
    return n


def _params(sem):
    return pltpu.CompilerParams(dimension_semantics=sem, vmem_limit_bytes=VMEM_LIMIT)


def _adaln_kernel(c_ref, w_ref, b_ref, o_ref):
    s = _silu(c_ref[...]).astype(BF16)
    o_ref[...] = _dot(s, w_ref[...].astype(BF16)) + b_ref[...]


def _adaln(cond, ada_w, ada_b):
    depth, d, n = ada_w.shape
    rows = cond.shape[0]
    tn = _pick_tile(n, (1024, 512, 256, 128))
    return pl.pallas_call(
        _adaln_kernel,
        grid=(depth, n // tn),
        in_specs=[pl.BlockSpec((rows, d), lambda l, j: (0, 0)),
                  pl.BlockSpec((None, d, tn), lambda l, j: (l, 0, j)),
                  pl.BlockSpec((None, 1, tn), lambda l, j: (l, 0, j))],
        out_specs=pl.BlockSpec((None, rows, tn), lambda l, j: (l, 0, j)),
        out_shape=jax.ShapeDtypeStruct((depth, rows, n), F32),
        compiler_params=_params(("parallel", "parallel")),
    )(cond, ada_w, ada_b.reshape(depth, 1, n))


def _modulated(x, nw, shift, scale):
    y = x * lax.rsqrt(jnp.mean(x * x, axis=-1, keepdims=True) + EPS) * nw
    return y * (1.0 + scale) + shift


def _ffn_kernel(x_ref, m_ref, nw_ref, wg_ref, wu_ref, wd_ref, o_ref, h_ref, *, base):
    j = pl.program_id(1)

    @pl.when(j == 0)
    def _():
        h = _modulated(x_ref[...], nw_ref[...], m_ref[base:base + 1, :], m_ref[base + 1:base + 2, :])
        h_ref[...] = h.astype(BF16)
        o_ref[...] = jnp.zeros_like(o_ref)

    h = h_ref[...]
    g = _dot(h, wg_ref[...])
    u = _dot(h, wu_ref[...])
    o_ref[...] += _dot((_silu(g) * u).astype(BF16), wd_ref[...])

    @pl.when(j == pl.num_programs(1) - 1)
    def _():
        o_ref[...] = x_ref[...] + 0.5 * m_ref[base + 2:base + 3, :] * o_ref[...]


def _cond_index(i, tm, n_ctx_rows, seq_rows):
    r = i * tm
    return jnp.where(r < n_ctx_rows, 0, (r - n_ctx_rows) // seq_rows + 1)


def _ffn(x, mod, nw, wg, wu, wd, base, geo):
    nt, d = x.shape
    f = wg.shape[1]
    tm = _pick_tile(geo.tile_rows, (512, 256, 128))
    tf = _pick_tile(f, (512, 256, 128))
    cidx = functools.partial(_cond_index, tm=tm, n_ctx_rows=geo.n_ctx_rows, seq_rows=geo.dec_seq)
    return pl.pallas_call(
        functools.partial(_ffn_kernel, base=base),
        grid=(nt // tm, f // tf),
        in_specs=[pl.BlockSpec((tm, d), lambda i, j: (i, 0)),
                  pl.BlockSpec((None, 9, d), lambda i, j: (cidx(i), 0, 0)),
                  pl.BlockSpec((1, d), lambda i, j: (0, 0)),
                  pl.BlockSpec((d, tf), lambda i, j: (0, j)),
                  pl.BlockSpec((d, tf), lambda i, j: (0, j)),
                  pl.BlockSpec((tf, d), lambda i, j: (j, 0))],
        out_specs=pl.BlockSpec((tm, d), lambda i, j: (i, 0)),
        out_shape=jax.ShapeDtypeStruct((nt, d), F32),
        scratch_shapes=[pltpu.VMEM((tm, d), BF16)],
        compiler_params=_params(("parallel", "arbitrary")),
    )(x, mod, nw, wg, wu, wd)


def _inproj_kernel(x_ref, m_ref, nw_ref, w_ref, o_ref, h_ref, *, base):
    @pl.when(pl.program_id(1) == 0)
    def _():
        h = _modulated(x_ref[...], nw_ref[...], m_ref[base:base + 1, :], m_ref[base + 1:base + 2, :])
        h_ref[...] = h.astype(BF16)

    o_ref[...] = _dot(h_ref[...], w_ref[...])


def _inproj(x, mod, nw, w, base, geo):
    nt, d = x.shape
    n = w.shape[1]
    tm = _pick_tile(geo.tile_rows, (512, 256, 128))
    tn = _pick_tile(n, (768, 512, 256, 128))
    cidx = functools.partial(_cond_index, tm=tm, n_ctx_rows=geo.n_ctx_rows, seq_rows=geo.dec_seq)
    return pl.pallas_call(
        functools.partial(_inproj_kernel, base=base),
        grid=(nt // tm, n // tn),
        in_specs=[pl.BlockSpec((tm, d), lambda i, j: (i, 0)),
                  pl.BlockSpec((None, 9, d), lambda i, j: (cidx(i), 0, 0)),
                  pl.BlockSpec((1, d), lambda i, j: (0, 0)),
                  pl.BlockSpec((d, tn), lambda i, j: (0, j))],
        out_specs=pl.BlockSpec((tm, tn), lambda i, j: (i, j)),
        out_shape=jax.ShapeDtypeStruct((nt, n), F32),
        scratch_shapes=[pltpu.VMEM((tm, d), BF16)],
        compiler_params=_params(("parallel", "arbitrary")),
    )(x, mod, nw, w)


def _merge_kernel(ba_ref, bh_ref, bg_ref, ga_ref, gh_ref, gg_ref, wb_ref, o_ref):
    acc = _sigmoid(ga_ref[...]) * _dot(ba_ref[...], wb_ref[0])
    acc += _sigmoid(gh_ref[...]) * _dot(bh_ref[...], wb_ref[1])
    acc += _sigmoid(gg_ref[...]) * _dot(bg_ref[...], wb_ref[2])
    o_ref[...] = acc.astype(BF16)


def _merge(br_a, br_h, br_g, proj, wb, geo):
    nt, bw = br_a.shape
    d = wb.shape[2]
    tm = _pick_tile(geo.tile_rows, (512, 256, 128))
    tn = _pick_tile(d, (512, 256, 128))
    nj = d // tn
    bspec = pl.BlockSpec((tm, bw), lambda i, j: (i, 0))
    return pl.pallas_call(
        _merge_kernel,
        grid=(nt // tm, nj),
        in_specs=[bspec, bspec, bspec,
                  pl.BlockSpec((tm, tn), lambda i, j: (i, j)),
                  pl.BlockSpec((tm, tn), lambda i, j: (i, j + nj)),
                  pl.BlockSpec((tm, tn), lambda i, j: (i, j + 2 * nj)),
                  pl.BlockSpec((3, bw, tn), lambda i, j: (0, 0, j))],
        out_specs=pl.BlockSpec((tm, tn), lambda i, j: (i, j)),
        out_shape=jax.ShapeDtypeStruct((nt, d), BF16),
        compiler_params=_params(("parallel", "parallel")),
    )(br_a, br_h, br_g, proj, proj, proj, wb)


def _outproj_kernel(a_ref, w_ref, x_ref, m_ref, o_ref, *, row):
    o_ref[...] = x_ref[...] + m_ref[row:row + 1, :] * _dot(a_ref[...], w_ref[...])


def _outproj(a, w, x, mod, row, geo):
    nt, d = x.shape
    k = a.shape[1]
    tm = _pick_tile(geo.tile_rows, (512, 256, 128))
    tn = _pick_tile(d, (512, 256, 128))
    cidx = functools.partial(_cond_index, tm=tm, n_ctx_rows=geo.n_ctx_rows, seq_rows=geo.dec_seq)
    return pl.pallas_call(
        functools.partial(_outproj_kernel, row=row),
        grid=(nt // tm, d // tn),
        in_specs=[pl.BlockSpec((tm, k), lambda i, j: (i, 0)),
                  pl.BlockSpec((k, tn), lambda i, j: (0, j)),
                  pl.BlockSpec((tm, tn), lambda i, j: (i, j)),
                  pl.BlockSpec((None, 9, tn), lambda i, j: (cidx(i), 0, j))],
        out_specs=pl.BlockSpec((tm, tn), lambda i, j: (i, j)),
        out_shape=jax.ShapeDtypeStruct((nt, d), F32),
        compiler_params=_params(("parallel", "parallel")),
    )(a, w, x, mod)


def _half_mean_matrix(da):
    m = np.kron(np.eye(2), np.ones((da, da))) / da
    return jnp.asarray(m, BF16)


def _qk_normed(x, gmat, gain):
    sq = x * x
    hi = sq.astype(BF16)
    lo = (sq - hi.astype(F32)).astype(BF16)
    msq = _dot(hi, gmat) + _dot(lo, gmat)
    return x * lax.rsqrt(msq + EPS) * gain


def _rotated(x, cos, sin_signed, da):
    q = da // 4
    width = x.shape[-1]
    lane = lax.broadcasted_iota(jnp.int32, x.shape, 1)
    first = (lane % (2 * q)) < q
    partner = jnp.where(first, pltpu.roll(x, width - q, axis=1), pltpu.roll(x, q, axis=1))
    return x * cos + partner * sin_signed


def _attn_prep_kernel(*refs, rope, emit_f32, da):
    if rope:
        q_ref, k_ref, v_ref, gm_ref, gn_ref, cos_ref, sin_ref = refs[:7]
        outs = refs[7:]
    else:
        q_ref, k_ref, v_ref, gm_ref, gn_ref = refs[:5]
        outs = refs[5:]
    gm = gm_ref[...]
    q = _qk_normed(q_ref[...], gm, gn_ref[0:1, :])
    k = _qk_normed(k_ref[...], gm, gn_ref[1:2, :])
    if rope:
        q = _rotated(q, cos_ref[...], sin_ref[...], da)
        k = _rotated(k, cos_ref[...], sin_ref[...], da)
    v = v_ref[...]
    outs[0][...] = (q * (da ** -0.5)).astype(BF16)
    outs[1][...] = k.astype(BF16)
    outs[2][...] = v.astype(BF16)
    if emit_f32:
        outs[3][...] = k
        outs[4][...] = v


def _attn_prep(proj, cols, gains, rope_tabs, nb, t, heads, da, row0, emit_f32):
    w = 2 * da
    tt = _pick_tile(t, (512, 256, 128))
    nblk = t // tt
    rb0 = row0 // tt
    cq, ck, cv = (c // w for c in cols)
    rope = rope_tabs is not None

    def col(c0):
        return pl.BlockSpec((tt, w), lambda b, h, i: (rb0 + b * nblk + i, c0 + h))

    in_specs = [col(cq), col(ck), col(cv),
                pl.BlockSpec((w, w), lambda b, h, i: (0, 0)),
                pl.BlockSpec((2, w), lambda b, h, i: (0, 0))]
    args = [proj, proj, proj, _half_mean_matrix(da), gains]
    if rope:
        in_specs += [pl.BlockSpec((tt, w), lambda b, h, i: (i, 0))] * 2
        args += list(rope_tabs)
    ospec = pl.BlockSpec((None, None, tt, w), lambda b, h, i: (b, h, i, 0))
    shape = (nb, heads, t, w)
    out_shape = [jax.ShapeDtypeStruct(shape, BF16)] * 3
    if emit_f32:
        out_shape += [jax.ShapeDtypeStruct(shape, F32)] * 2
    return pl.pallas_call(
        functools.partial(_attn_prep_kernel, rope=rope, emit_f32=emit_f32, da=da),
        grid=(nb, heads, nblk),
        in_specs=in_specs,
        out_specs=[ospec] * len(out_shape),
        out_shape=out_shape,
        compiler_params=_params(("parallel", "parallel", "parallel")),
    )(*args)


def _attn_kernel(q_ref, k_ref, v_ref, lam_ref, g_ref, o_ref, m_ref, l_ref, acc_ref, *, tk, lam_init, da):
    tq = q_ref.shape[0]
    nk = k_ref.shape[0] // tk
    q = q_ref[...]
    lane = lax.broadcasted_iota(jnp.int32, q.shape, 1)
    zero = jnp.zeros_like(q)
    q2 = jnp.concatenate([jnp.where(lane < da, q, zero), jnp.where(lane >= da, q, zero)], axis=0)
    m_ref[...] = jnp.full_like(m_ref, -jnp.inf)
    l_ref[...] = jnp.zeros_like(l_ref)
    acc_ref[...] = jnp.zeros_like(acc_ref)

    def body(i, carry):
        start = pl.multiple_of(i * tk, tk)
        k = k_ref[pl.ds(start, tk), :]
        v = v_ref[pl.ds(start, tk), :]
        s = _dot_nt(q2, k)
        m_old = m_ref[...]
        m_new = jnp.maximum(m_old, jnp.max(s, axis=-1, keepdims=True))
        alpha = jnp.exp(m_old - m_new)
        p = jnp.exp(s - m_new)
        l_ref[...] = alpha * l_ref[...] + jnp.sum(p, axis=-1, keepdims=True)
        acc_ref[...] = alpha * acc_ref[...] + _dot(p.astype(BF16), v)
        m_ref[...] = m_new
        return carry

    lax.fori_loop(0, nk, body, 0)

    lam_p = lam_ref[...]
    lam = (jnp.exp(jnp.sum(lam_p[0:1, :] * lam_p[1:2, :], axis=-1, keepdims=True))
           - jnp.exp(jnp.sum(lam_p[2:3, :] * lam_p[3:4, :], axis=-1, keepdims=True)) + lam_init)
    o = acc_ref[...] / l_ref[...]
    o = o[:tq, :] - lam * o[tq:, :]
    o = o * lax.rsqrt(jnp.mean(o * o, axis=-1, keepdims=True) + EPS) * g_ref[...]
    o_ref[...] = (o * (1.0 - lam_init)).astype(BF16)


def _attention(q, k, v, lam_p, gain, lam_init, da):
    nb, heads, n, w = q.shape
    m = k.shape[2]
    tq = _pick_tile(n, (256, 128))
    tk = _pick_tile(m, (256, 128))
    nq = n // tq
    kv = pl.BlockSpec((None, None, m, w), lambda b, h, i: (b, h, 0, 0))
    return pl.pallas_call(
        functools.partial(_attn_kernel, tk=tk, lam_init=lam_init, da=da),
        grid=(nb, heads, nq),
        in_specs=[pl.BlockSpec((None, None, tq, w), lambda b, h, i: (b, h, i, 0)), kv, kv,
                  pl.BlockSpec(lam_p.shape, lambda b, h, i: (0, 0)),
                  pl.BlockSpec((1, w), lambda b, h, i: (0, 0))],
        out_specs=pl.BlockSpec((tq, w), lambda b, h, i: (b * nq + i, h)),
        out_shape=jax.ShapeDtypeStruct((nb * n, heads * w), BF16),
        scratch_shapes=[pltpu.VMEM((2 * tq, 1), F32), pltpu.VMEM((2 * tq, 1), F32),
                        pltpu.VMEM((2 * tq, w), F32)],
        compiler_params=_params(("parallel", "parallel", "parallel")),
    )(q, k, v, lam_p, gain)


def _scan_constants(tb, c, reverse):
    n = tb // c
    i = np.arange(c)[:, None]
    t = np.arange(c)[None, :]
    cum = ((t >= i) if reverse else (t <= i)).astype(np.float32)
    ref_row = c // 2 if reverse else c // 2 - 1
    end_row = 0 if reverse else c - 1
    rel = cum - cum[ref_row][None, :]
    tail = cum[end_row][None, :] - cum
    eye = np.eye(n, dtype=np.float32)
    fast = np.concatenate([np.kron(eye, cum), np.kron(eye, rel), np.kron(eye, tail)], axis=0)

    levels, masks = [], []
    g = c
    while g >= 2:
        h = g // 2
        mat = np.zeros((c, c), np.float32)
        for r in range(c):
            mid = (r // g) * g + h
            if not reverse:
                if r >= mid:
                    mat[r, mid:r + 1] = 1.0
                else:
                    mat[r, r + 1:mid] = 1.0
            else:
                if r < mid:
                    mat[r, r:mid] = 1.0
                else:
                    mat[r, mid:r] = 1.0
        same = (i // g) == (t // g)
        if not reverse:
            ok = same & ((i % g) >= h) & ((t % g) < h)
        else:
            ok = same & ((i % g) < h) & ((t % g) >= h)
        levels.append(mat)
        masks.append(ok.astype(np.float32))
        g = h
    masks.append(np.eye(c, dtype=np.float32))
    return (jnp.asarray(fast, BF16), jnp.asarray(np.concatenate(levels, axis=0), BF16),
            jnp.asarray(np.stack(masks), F32))


def _scan_block(q, k, v, la, fast_ref, slow_ref, mask_ref, st_ref, o_ref, *, c, reverse):
    tb = q.shape[0]
    n = tb // c
    parts = _split3(la)
    z = _dot_exact_lhs(fast_ref[...], parts)
    b, zq, zu = z[:tb], z[tb:2 * tb], z[2 * tb:]
    safe = jnp.max(jnp.abs(zq)) < SAFE_EXPONENT
    q_in = (q * jnp.exp(b)).astype(BF16)
    k_up = (k * jnp.exp(zu)).astype(BF16)
    q_mid = (q * jnp.exp(zq)).astype(BF16)
    k_mid = (k * jnp.exp(-zq)).astype(BF16)
    q_bf = q.astype(BF16)
    k_bf = k.astype(BF16)
    v_bf = v.astype(BF16)
    row = lax.broadcasted_iota(jnp.int32, (c, c), 0)
    col = lax.broadcasted_iota(jnp.int32, (c, c), 1)
    visible = (col >= row) if reverse else (col <= row)
    n_levels = slow_ref.shape[0] // c

    for ci in (range(n - 1, -1, -1) if reverse else range(n)):
        sl = slice(ci * c, (ci + 1) * c)

        def fast_scores(sl=sl):
            return jnp.where(visible, _dot_nt(q_mid[sl], k_mid[sl]), 0.0)

        def slow_scores(sl=sl):
            zs = _dot_exact_lhs(slow_ref[...], tuple(p[sl] for p in parts))
            a = mask_ref[n_levels] * _dot_nt(q_bf[sl], k_bf[sl])
            for lv in range(n_levels):
                e = jnp.exp(zs[lv * c:(lv + 1) * c])
                a += mask_ref[lv] * _dot_nt((q[sl] * e).astype(BF16), (k[sl] * e).astype(BF16))
            return a

        a = lax.cond(safe, fast_scores, slow_scores)
        st = st_ref[...]
        o_ref[sl, :] = _dot_nt(q_in[sl], st.astype(BF16)) + _dot(a.astype(BF16), v_bf[sl])
        end = ci * c if reverse else (ci + 1) * c - 1
        st_ref[...] = st * jnp.exp(b[end:end + 1, :]) + _dot_tn(v_bf[sl], k_up[sl])


def _scan_finish(o_ref, prev_ref, gate_ref, gn_ref, y_ref):
    o = o_ref[...] + prev_ref[...]
    o = o * lax.rsqrt(jnp.mean(o * o, axis=-1, keepdims=True) + EPS) * gn_ref[...]
    y_ref[...] = (o * _silu(gate_ref[...])).astype(BF16)


def _scan_state_io(s0_ref, sT_ref, st_ref, do_body):
    t = pl.program_id(2)

    @pl.when(t == 0)
    def _():
        st_ref[...] = s0_ref[...].T

    do_body()

    @pl.when(t == pl.num_programs(2) - 1)
    def _():
        sT_ref[...] = st_ref[...].T


def _hgrn_kernel(*refs, c, reverse, last_pass):
    if last_pass:
        (q_ref, f_ref, i_ref, lb_ref, s0_ref, fast_ref, slow_ref, mask_ref, prev_ref, gate_ref, gn_ref,
         y_ref, sT_ref, st_ref, o_scr) = refs
    else:
        (q_ref, f_ref, i_ref, lb_ref, s0_ref, fast_ref, slow_ref, mask_ref,
         y_ref, sT_ref, st_ref) = refs
        o_scr = y_ref

    def body():
        lb = lb_ref[...]
        f = lb + (1.0 - lb) * _sigmoid(f_ref[...])
        _scan_block(_silu(q_ref[...]), 1.0 - f, i_ref[...], jnp.log(f),
                    fast_ref, slow_ref, mask_ref, st_ref, o_scr, c=c, reverse=reverse)
        if last_pass:
            _scan_finish(o_scr, prev_ref, gate_ref, gn_ref, y_ref)

    _scan_state_io(s0_ref, sT_ref, st_ref, body)


def _gla_kernel(*refs, c, reverse, last_pass, rank, q_scale):
    if last_pass:
        (q_ref, k_ref, v_ref, r_ref, wd_ref, bd_ref, s0_ref, fast_ref, slow_ref, mask_ref,
         prev_ref, gate_ref, gn_ref, y_ref, sT_ref, st_ref, o_scr) = refs
    else:
        (q_ref, k_ref, v_ref, r_ref, wd_ref, bd_ref, s0_ref, fast_ref, slow_ref, mask_ref,
         y_ref, sT_ref, st_ref) = refs
        o_scr = y_ref

    def body():
        lo = rank if reverse else 0
        r = r_ref[...][:, lo:lo + rank].astype(BF16)
        logit = _dot(r, wd_ref[...].astype(BF16)) + bd_ref[...]
        la = (jnp.minimum(logit, 0.0) - jnp.log(1.0 + jnp.exp(-jnp.abs(logit)))) / GLA_TAU
        _scan_block(q_ref[...] * q_scale, k_ref[...], v_ref[...], la,
                    fast_ref, slow_ref, mask_ref, st_ref, o_scr, c=c, reverse=reverse)
        if last_pass:
            _scan_finish(o_scr, prev_ref, gate_ref, gn_ref, y_ref)

    _scan_state_io(s0_ref, sT_ref, st_ref, body)


def _scan_call(kernel, proj, col_specs, extra, s0, prev, gate_col, gn, nb, t, heads, dk, dv, row0,
               reverse, **kw):
    tb = _pick_tile(t, (SCAN_BLOCK, 128))
    c = min(SCAN_CHUNK, tb)
    nblk = t // tb
    rb0 = row0 // tb
    last_pass = prev is not None
    fast, slow, masks = _scan_constants(tb, c, reverse)

    def tblk(i):
        return (nblk - 1 - i) if reverse else i

    def col(off, width, per_head=True):
        step = 1 if per_head else 0
        return pl.BlockSpec((tb, width), lambda b, h, i: (rb0 + b * nblk + tblk(i), off // width + step * h))

    in_specs = [col(*spec) for spec in col_specs]
    args = [proj] * len(col_specs)
    for e in extra:
        in_specs.append(pl.BlockSpec((None,) + e.shape[1:], lambda b, h, i: (h, 0, 0)))
        args.append(e)
    in_specs.append(pl.BlockSpec((None, None, dk, dv), lambda b, h, i: (b, h, 0, 0)))
    args.append(s0)
    for cst in (fast, slow, masks):
        in_specs.append(pl.BlockSpec(cst.shape, lambda b, h, i, nd=cst.ndim: (0,) * nd))
        args.append(cst)
    scratch = [pltpu.VMEM((dv, dk), F32)]
    if last_pass:
        in_specs += [pl.BlockSpec((tb, dv), lambda b, h, i: (b * nblk + tblk(i), h)),
                     col(gate_col, dv),
                     pl.BlockSpec((1, dv), lambda b, h, i: (0, 0))]
        args += [prev, proj, gn]
        scratch.append(pltpu.VMEM((tb, dv), F32))
    y_dtype = BF16 if last_pass else F32
    return pl.pallas_call(
        functools.partial(kernel, c=c, reverse=reverse, last_pass=last_pass, **kw),
        grid=(nb, heads, nblk),
        in_specs=in_specs,
        out_specs=[pl.BlockSpec((tb, dv), lambda b, h, i: (b * nblk + tblk(i), h)),
                   pl.BlockSpec((None, None, dk, dv), lambda b, h, i: (b, h, 0, 0))],
        out_shape=[jax.ShapeDtypeStruct((nb * t, heads * dv), y_dtype),
                   jax.ShapeDtypeStruct((nb, heads, dk, dv), F32)],
        scratch_shapes=scratch,
        compiler_params=_params(("parallel", "parallel", "arbitrary")),
    )(*args)


class _Geometry:
    pass


def _rope_tables(n_tok, da):
    ax = da // 2
    rows = n_tok // GRID_W
    r = jnp.broadcast_to(jnp.arange(rows, dtype=F32)[:, None], (rows, GRID_W)).reshape(-1)
    col = jnp.broadcast_to(jnp.arange(GRID_W, dtype=F32)[None, :], (rows, GRID_W)).reshape(-1)
    inv = ROPE_BASE ** (-jnp.arange(0, ax, 2, dtype=F32) / ax)
    ar, ac = r[:, None] * inv, col[:, None] * inv
    ang = jnp.concatenate([ar, ar, ac, ac], axis=-1)
    cos, sin = jnp.cos(ang), jnp.sin(ang)
    q = da // 4
    sign = jnp.where((jnp.arange(da) % (2 * q)) < q, -1.0, 1.0).astype(F32)
    return jnp.tile(cos, (1, 2)), jnp.tile(sin * sign, (1, 2))


def kernel(x_prompt, x_sample, c, cache_attn_k, cache_attn_v, state_hgrn, state_gla, c_ctx, ada_w, ada_b, norm_w, ffn_w_gate, ffn_w_up, ffn_w_down, w_in, attn_qk_norm, attn_lambda, attn_out_norm, hgrn_lb_logits, hgrn_out_norm, gla_w_decay, gla_b_decay, gla_out_norm, w_branch, w_out):
    batch, seq, d = x_prompt.shape
    dec_batch, dec_seq, _ = x_sample.shape
    depth = ada_w.shape[0]
    ha, da = cache_attn_k.shape[2], cache_attn_k.shape[5]
    dva = 2 * da
    hh, hdk, hdv = state_hgrn.shape[3:]
    hg, gdk, gdv = state_gla.shape[3:]
    rank = gla_w_decay.shape[2]
    sizes = (ha * 2 * da, ha * 2 * da, ha * dva, hh * hdk, hh * hdk, hh * hdk, hh * hdv, hh * hdv,
             hg * gdk, hg * gdk, hg * gdv, hg * gdv, 2 * rank, 3 * d)
    assert sum(sizes) == w_in.shape[2] and hdk == LANES and hdv == LANES and gdk == LANES and dva == LANES

    geo = _Geometry()
    geo.n_ctx_rows = batch * seq
    geo.dec_seq = dec_seq
    geo.tile_rows = math.gcd(geo.n_ctx_rows, dec_seq)
    n_ctx = geo.n_ctx_rows

    src = np.concatenate([[0], np.cumsum(sizes)])
    n_main = int(src[12])
    glr_pad = 2 * LANES
    names = ('aq', 'ak', 'av', 'hq', 'hf_f', 'hf_b', 'hi', 'hg', 'gq', 'gk', 'gv', 'gg')
    off = {nm: 3 * d + int(src[i]) for i, nm in enumerate(names)}
    off['glr'] = 3 * d + n_main

    def pack_w_in(w):
        return jnp.concatenate([w[:, src[13]:], w[:, :n_main], w[:, src[12]:src[13]],
                                jnp.zeros((d, glr_pad - 2 * rank), w.dtype)], axis=1).astype(BF16)

    n_cond = 1 + dec_batch
    cond = jnp.concatenate([c_ctx[None], c, jnp.zeros((-n_cond % 8, d), F32)], axis=0)
    mod = _adaln(cond, ada_w, ada_b).reshape(depth, cond.shape[0], 9, d)

    lb_w = jax.nn.softmax(hgrn_lb_logits.astype(F32), axis=0)
    lower_bounds = jnp.cumsum(lb_w, axis=0) - lb_w[0]
    rope_tabs = _rope_tables(dec_seq, da)

    x = jnp.concatenate([x_prompt.reshape(n_ctx, d), x_sample.reshape(dec_batch * dec_seq, d)], axis=0)
    zeros_h = jnp.zeros((batch, hh, hdk, hdv), F32)
    zeros_g = jnp.zeros((batch, hg, gdk, gdv), F32)
    ks_l, vs_l, sh_l, sg_l = [], [], [], []

    for l in range(depth):
        m_l = mod[l]
        nw = norm_w[l]
        x = _ffn(x, m_l, nw[0:1], ffn_w_gate[l, 0].astype(BF16), ffn_w_up[l, 0].astype(BF16),
                 ffn_w_down[l, 0].astype(BF16), 0, geo)
        proj = _inproj(x, m_l, nw[1:2], pack_w_in(w_in[l]), 3, geo)

        lam_init = 0.8 - 0.6 * math.exp(-0.3 * l)
        gains = jnp.tile(attn_qk_norm[l], (1, 2))
        cols = (off['aq'], off['ak'], off['av'])
        gn_a = attn_out_norm[l][None]
        qp, kp, vp, k32, v32 = _attn_prep(proj, cols, gains, None, batch, seq, ha, da, 0, True)
        ks_l.append(k32.reshape(batch, ha, seq, 2, da))
        vs_l.append(v32)
        qs, ksm, vsm = _attn_prep(proj, cols, gains, rope_tabs, dec_batch, dec_seq, ha, da, n_ctx, False)
        k_all = jnp.concatenate([cache_attn_k[:, l].reshape(dec_batch, ha, -1, 2 * da).astype(BF16), ksm], axis=2)
        v_all = jnp.concatenate([cache_attn_v[:, l].astype(BF16), vsm], axis=2)
        br_a = jnp.concatenate([_attention(qp, kp, vp, attn_lambda[l], gn_a, lam_init, da),
                                _attention(qs, k_all, v_all, attn_lambda[l], gn_a, lam_init, da)], axis=0)

        lb = lower_bounds[l].reshape(2, hh, 1, hdk)
        gn_h = hgrn_out_norm[l][None]

        def hgrn(nb, t, row0, s0):
            specs_b = ((off['hq'], hdk), (off['hf_b'], hdk), (off['hi'], hdv))
            specs_f = ((off['hq'], hdk), (off['hf_f'], hdk), (off['hi'], hdv))
            o_b, s_b = _scan_call(_hgrn_kernel, proj, specs_b, [lb[1]], s0[1], None, None, None,
                                  nb, t, hh, hdk, hdv, row0, True)
            y, s_f = _scan_call(_hgrn_kernel, proj, specs_f, [lb[0]], s0[0], o_b, off['hg'], gn_h,
                                nb, t, hh, hdk, hdv, row0, False)
            return y, jnp.stack([s_f, s_b], axis=1)

        y_p, s_h = hgrn(batch, seq, 0, (zeros_h, zeros_h))
        y_s, _ = hgrn(dec_batch, dec_seq, n_ctx, (state_hgrn[:, l, 0], state_hgrn[:, l, 1]))
        br_h = jnp.concatenate([y_p, y_s], axis=0)
        sh_l.append(s_h)

        wd = gla_w_decay[l].reshape(2, rank, hg, gdk).transpose(0, 2, 1, 3)
        bd = gla_b_decay[l].reshape(2, hg, 1, gdk)
        gn_g = gla_out_norm[l][None]

        def gla(nb, t, row0, s0):
            specs = ((off['gq'], gdk), (off['gk'], gdk), (off['gv'], gdv), (off['glr'], LANES, False))
            kw = dict(rank=rank, q_scale=gdk ** -0.5)
            o_b, s_b = _scan_call(_gla_kernel, proj, specs, [wd[1], bd[1]], s0[1], None, None, None,
                                  nb, t, hg, gdk, gdv, row0, True, **kw)
            y, s_f = _scan_call(_gla_kernel, proj, specs, [wd[0], bd[0]], s0[0], o_b, off['gg'], gn_g,
                                nb, t, hg, gdk, gdv, row0, False, **kw)
            return y, jnp.stack([s_f, s_b], axis=1)

        y_p, s_g = gla(batch, seq, 0, (zeros_g, zeros_g))
        y_s, _ = gla(dec_batch, dec_seq, n_ctx, (state_gla[:, l, 0], state_gla[:, l, 1]))
        br_g = jnp.concatenate([y_p, y_s], axis=0)
        sg_l.append(s_g)

        merged = _merge(br_a, br_h, br_g, proj, w_branch[l].astype(BF16), geo)
        x = _outproj(merged, w_out[l].astype(BF16), x, m_l, 5, geo)
        x = _ffn(x, m_l, nw[2:3], ffn_w_gate[l, 1].astype(BF16), ffn_w_up[l, 1].astype(BF16),
                 ffn_w_down[l, 1].astype(BF16), 6, geo)

    y_prompt = x[:n_ctx].reshape(batch, seq, d)
    y_sample = x[n_ctx:].reshape(dec_batch, dec_seq, d)
    return (y_prompt, y_sample, jnp.stack(ks_l, axis=1), jnp.stack(vs_l, axis=1),
            jnp.stack(sh_l, axis=1), jnp.stack(sg_l, axis=1))
```

```python
import functools
import math

import jax
import jax.numpy as jnp
import numpy as np
from jax import lax
from jax.experimental import pallas as pl
from jax.experimental.pallas import tpu as pltpu

F32 = jnp.float32
BF16 = jnp.bfloat16

GRID_W = 64
GLA_TAU = 16.0
ROPE_BASE = 10000.0
EPS = 1e-6
LANES = 128
SCAN_CHUNK = 64
SCAN_BLOCK = 256
ATTN_LOOKAHEAD = 2
SAFE_EXPONENT = 80.0
VMEM_LIMIT = 56 * 1024 * 1024


def _dot(a, b):
    return jnp.dot(a, b, preferred_element_type=F32)


def _dot_nt(a, b):
    return lax.dot_general(a, b, (((1,), (1,)), ((), ())), preferred_element_type=F32)


def _dot_tn(a, b):
    return lax.dot_general(a, b, (((0,), (0,)), ((), ())), preferred_element_type=F32)


def _sigmoid(x):
    return 1.0 / (1.0 + jnp.exp(-x))


def _silu(x):
    return x * _sigmoid(x)


def _split3(x):
    hi = x.astype(BF16)
    r1 = x - hi.astype(F32)
    mid = r1.astype(BF16)
    lo = (r1 - mid.astype(F32)).astype(BF16)
    return hi, mid, lo


def _dot_exact_lhs(m, parts):
    return _dot(m, parts[0]) + _dot(m, parts[1]) + _dot(m, parts[2])


def _pick_tile(n, candidates):
    for c in candidates:
        if n % c == 0:
            return c
    return n


def _params(sem):
    return pltpu.CompilerParams(dimension_semantics=sem, vmem_limit_bytes=VMEM_LIMIT)


def _adaln_kernel(c_ref, w_ref, b_ref, o_ref):
    s = _silu(c_ref[...]).astype(BF16)
    o_ref[...] = _dot(s, w_ref[...].astype(BF16)) + b_ref[...]


def _adaln(cond, ada_w, ada_b):
    depth, d, n = ada_w.shape
    rows = cond.shape[0]
    tn = _pick_tile(n, (1024, 512, 256, 128))
    return pl.pallas_call(
        _adaln_kernel,
        grid=(depth, n // tn),
        in_specs=[pl.BlockSpec((rows, d), lambda l, j: (0, 0)),
                  pl.BlockSpec((None, d, tn), lambda l, j: (l, 0, j)),
                  pl.BlockSpec((None, 1, tn), lambda l, j: (l, 0, j))],
        out_specs=pl.BlockSpec((None, rows, tn), lambda l, j: (l, 0, j)),
        out_shape=jax.ShapeDtypeStruct((depth, rows, n), F32),
        compiler_params=_params(("parallel", "parallel")),
    )(cond, ada_w, ada_b.reshape(depth, 1, n))


def _modulated(x, nw, shift, scale):
    y = x * lax.rsqrt(jnp.mean(x * x, axis=-1, keepdims=True) + EPS) * nw
    return y * (1.0 + scale) + shift


def _ffn_kernel(x_ref, m_ref, nw_ref, wg_ref, wu_ref, wd_ref, o_ref, h_ref, *, base):
    j = pl.program_id(1)

    @pl.when(j == 0)
    def _():
        h = _modulated(x_ref[...], nw_ref[...], m_ref[base:base + 1, :], m_ref[base + 1:base + 2, :])
        h_ref[...] = h.astype(BF16)
        o_ref[...] = jnp.zeros_like(o_ref)

    h = h_ref[...]
    g = _dot(h, wg_ref[...])
    u = _dot(h, wu_ref[...])
    o_ref[...] += _dot((_silu(g) * u).astype(BF16), wd_ref[...])

    @pl.when(j == pl.num_programs(1) - 1)
    def _():
        o_ref[...] = x_ref[...] + 0.5 * m_ref[base + 2:base + 3, :] * o_ref[...]


def _cond_index(i, tm, n_ctx_rows, seq_rows):
    r = i * tm
    return jnp.where(r < n_ctx_rows, 0, (r - n_ctx_rows) // seq_rows + 1)


def _ffn(x, mod, nw, wg, wu, wd, base, geo):
    nt, d = x.shape
    f = wg.shape[1]
    tm = _pick_tile(geo.tile_rows, (512, 256, 128))
    tf = _pick_tile(f, (512, 256, 128))
    cidx = functools.partial(_cond_index, tm=tm, n_ctx_rows=geo.n_ctx_rows, seq_rows=geo.dec_seq)
    return pl.pallas_call(
        functools.partial(_ffn_kernel, base=base),
        grid=(nt // tm, f // tf),
        in_specs=[pl.BlockSpec((tm, d), lambda i, j: (i, 0)),
                  pl.BlockSpec((None, 9, d), lambda i, j: (cidx(i), 0, 0)),
                  pl.BlockSpec((1, d), lambda i, j: (0, 0)),
                  pl.BlockSpec((d, tf), lambda i, j: (0, j)),
                  pl.BlockSpec((d, tf), lambda i, j: (0, j)),
                  pl.BlockSpec((tf, d), lambda i, j: (j, 0))],
        out_specs=pl.BlockSpec((tm, d), lambda i, j: (i, 0)),
        out_shape=jax.ShapeDtypeStruct((nt, d), F32),
        scratch_shapes=[pltpu.VMEM((tm, d), BF16)],
        compiler_params=_params(("parallel", "arbitrary")),
    )(x, mod, nw, wg, wu, wd)


def _inproj_kernel(x_ref, m_ref, nw_ref, w_ref, o_ref, h_ref, *, base):
    @pl.when(pl.program_id(1) == 0)
    def _():
        h = _modulated(x_ref[...], nw_ref[...], m_ref[base:base + 1, :], m_ref[base + 1:base + 2, :])
        h_ref[...] = h.astype(BF16)

    o_ref[...] = _dot(h_ref[...], w_ref[...])


def _inproj(x, mod, nw, w, base, geo):
    nt, d = x.shape
    n = w.shape[1]
    tm = _pick_tile(geo.tile_rows, (1024, 512, 256, 128))
    tn = _pick_tile(n, (768, 512, 256, 128))
    cidx = functools.partial(_cond_index, tm=tm, n_ctx_rows=geo.n_ctx_rows, seq_rows=geo.dec_seq)
    return pl.pallas_call(
        functools.partial(_inproj_kernel, base=base),
        grid=(nt // tm, n // tn),
        in_specs=[pl.BlockSpec((tm, d), lambda i, j: (i, 0)),
                  pl.BlockSpec((None, 9, d), lambda i, j: (cidx(i), 0, 0)),
                  pl.BlockSpec((1, d), lambda i, j: (0, 0)),
                  pl.BlockSpec((d, tn), lambda i, j: (0, j))],
        out_specs=pl.BlockSpec((tm, tn), lambda i, j: (i, j)),
        out_shape=jax.ShapeDtypeStruct((nt, n), F32),
        scratch_shapes=[pltpu.VMEM((tm, d), BF16)],
        compiler_params=_params(("parallel", "arbitrary")),
    )(x, mod, nw, w)


def _merge_kernel(ba_ref, bh_ref, bg_ref, ga_ref, gh_ref, gg_ref, wb_ref, o_ref):
    acc = _sigmoid(ga_ref[...]) * _dot(ba_ref[...], wb_ref[0])
    acc += _sigmoid(gh_ref[...]) * _dot(bh_ref[...], wb_ref[1])
    acc += _sigmoid(gg_ref[...]) * _dot(bg_ref[...], wb_ref[2])
    o_ref[...] = acc.astype(BF16)


def _merge(br_a, br_h, br_g, proj, wb, geo):
    nt, bw = br_a.shape
    d = wb.shape[2]
    tm = _pick_tile(geo.tile_rows, (512, 256, 128))
    tn = _pick_tile(d, (512, 256, 128))
    nj = d // tn
    bspec = pl.BlockSpec((tm, bw), lambda i, j: (i, 0))
    return pl.pallas_call(
        _merge_kernel,
        grid=(nt // tm, nj),
        in_specs=[bspec, bspec, bspec,
                  pl.BlockSpec((tm, tn), lambda i, j: (i, j)),
                  pl.BlockSpec((tm, tn), lambda i, j: (i, j + nj)),
                  pl.BlockSpec((tm, tn), lambda i, j: (i, j + 2 * nj)),
                  pl.BlockSpec((3, bw, tn), lambda i, j: (0, 0, j))],
        out_specs=pl.BlockSpec((tm, tn), lambda i, j: (i, j)),
        out_shape=jax.ShapeDtypeStruct((nt, d), BF16),
        compiler_params=_params(("parallel", "parallel")),
    )(br_a, br_h, br_g, proj, proj, proj, wb)


def _outproj_kernel(a_ref, w_ref, x_ref, m_ref, o_ref, *, row):
    o_ref[...] = x_ref[...] + m_ref[row:row + 1, :] * _dot(a_ref[...], w_ref[...])


def _outproj(a, w, x, mod, row, geo):
    nt, d = x.shape
    k = a.shape[1]
    tm = _pick_tile(geo.tile_rows, (512, 256, 128))
    tn = _pick_tile(d, (512, 256, 128))
    cidx = functools.partial(_cond_index, tm=tm, n_ctx_rows=geo.n_ctx_rows, seq_rows=geo.dec_seq)
    return pl.pallas_call(
        functools.partial(_outproj_kernel, row=row),
        grid=(nt // tm, d // tn),
        in_specs=[pl.BlockSpec((tm, k), lambda i, j: (i, 0)),
                  pl.BlockSpec((k, tn), lambda i, j: (0, j)),
                  pl.BlockSpec((tm, tn), lambda i, j: (i, j)),
                  pl.BlockSpec((None, 9, tn), lambda i, j: (cidx(i), 0, j))],
        out_specs=pl.BlockSpec((tm, tn), lambda i, j: (i, j)),
        out_shape=jax.ShapeDtypeStruct((nt, d), F32),
        compiler_params=_params(("parallel", "parallel")),
    )(a, w, x, mod)


def _half_mean_matrix(da):
    m = np.kron(np.eye(2), np.ones((da, da))) / da
    return jnp.asarray(m, BF16)


def _qk_normed(x, gmat, gain):
    sq = x * x
    hi = sq.astype(BF16)
    lo = (sq - hi.astype(F32)).astype(BF16)
    msq = _dot(hi, gmat) + _dot(lo, gmat)
    return x * lax.rsqrt(msq + EPS) * gain


def _rotated(x, cos, sin_signed, da):
    q = da // 4
    width = x.shape[-1]
    lane = lax.broadcasted_iota(jnp.int32, x.shape, 1)
    first = (lane % (2 * q)) < q
    partner = jnp.where(first, pltpu.roll(x, width - q, axis=1), pltpu.roll(x, q, axis=1))
    return x * cos + partner * sin_signed


def _attn_prep_kernel(*refs, rope, emit_f32, da):
    if rope:
        q_ref, k_ref, v_ref, gm_ref, gn_ref, cos_ref, sin_ref = refs[:7]
        outs = refs[7:]
    else:
        q_ref, k_ref, v_ref, gm_ref, gn_ref = refs[:5]
        outs = refs[5:]
    gm = gm_ref[...]
    q = _qk_normed(q_ref[...], gm, gn_ref[0:1, :])
    k = _qk_normed(k_ref[...], gm, gn_ref[1:2, :])
    if rope:
        q = _rotated(q, cos_ref[...], sin_ref[...], da)
        k = _rotated(k, cos_ref[...], sin_ref[...], da)
    v = v_ref[...]
    outs[0][...] = (q * (math.log2(math.e) * da ** -0.5)).T.astype(BF16)
    outs[1][...] = k.astype(BF16)
    outs[2][...] = v.T.astype(BF16)
    if emit_f32:
        outs[3][...] = k
        outs[4][...] = v


def _attn_prep(proj, cols, gains, rope_tabs, nb, t, heads, da, row0, emit_f32):
    w = 2 * da
    tt = _pick_tile(t, (512, 256, 128))
    nblk = t // tt
    rb0 = row0 // tt
    cq, ck, cv = (c // w for c in cols)
    rope = rope_tabs is not None

    def col(c0):
        return pl.BlockSpec((tt, w), lambda b, h, i: (rb0 + b * nblk + i, c0 + h))

    in_specs = [col(cq), col(ck), col(cv),
                pl.BlockSpec((w, w), lambda b, h, i: (0, 0)),
                pl.BlockSpec((2, w), lambda b, h, i: (0, 0))]
    args = [proj, proj, proj, _half_mean_matrix(da), gains]
    if rope:
        in_specs += [pl.BlockSpec((tt, w), lambda b, h, i: (i, 0))] * 2
        args += list(rope_tabs)
    rows = pl.BlockSpec((None, None, tt, w), lambda b, h, i: (b, h, i, 0))
    cols_t = pl.BlockSpec((None, None, w, tt), lambda b, h, i: (b, h, 0, i))
    out_specs = [cols_t, rows, cols_t]
    out_shape = [jax.ShapeDtypeStruct((nb, heads, w, t), BF16),
                 jax.ShapeDtypeStruct((nb, heads, t, w), BF16),
                 jax.ShapeDtypeStruct((nb, heads, w, t), BF16)]
    if emit_f32:
        out_specs += [rows, rows]
        out_shape += [jax.ShapeDtypeStruct((nb, heads, t, w), F32)] * 2
    return pl.pallas_call(
        functools.partial(_attn_prep_kernel, rope=rope, emit_f32=emit_f32, da=da),
        grid=(nb, heads, nblk),
        in_specs=in_specs,
        out_specs=out_specs,
        out_shape=out_shape,
        compiler_params=_params(("parallel", "parallel", "parallel")),
    )(*args)


def _attn_kernel(qt_ref, k_ref, vt_ref, lam_ref, g_ref, o_ref, *, tk, lam_init, da):
    nk = k_ref.shape[0] // tk
    qt = qt_ref[...]
    feat = lax.broadcasted_iota(jnp.int32, qt.shape, 0)
    zero = jnp.zeros_like(qt)

    tq = qt.shape[1]
    q2t = jnp.concatenate([jnp.where(feat < da, qt, zero), jnp.where(feat >= da, qt, zero)], axis=1)

    def scores(i):
        return _dot(k_ref[i * tk:(i + 1) * tk, :], q2t)

    ahead = [scores(i) for i in range(min(ATTN_LOOKAHEAD, nk))]
    m = l = acc = None
    for i in range(nk):
        s = ahead.pop(0)
        if i + ATTN_LOOKAHEAD < nk:
            ahead.append(scores(i + ATTN_LOOKAHEAD))
        m_tile = jnp.max(s, axis=0, keepdims=True)
        m_new = m_tile if i == 0 else jnp.maximum(m, m_tile)
        p = jnp.exp2(s - m_new)
        pv = _dot(vt_ref[:, i * tk:(i + 1) * tk], p.astype(BF16))
        if i == 0:
            l, acc = jnp.sum(p, axis=0, keepdims=True), pv
        else:
            alpha = jnp.exp2(m - m_new)
            l = alpha * l + jnp.sum(p, axis=0, keepdims=True)
            acc = alpha * acc + pv
        m = m_new
    o = acc / l
    lam_p = lam_ref[...]
    lam = (jnp.exp(jnp.sum(lam_p[0:1, :] * lam_p[1:2, :], axis=-1, keepdims=True))
           - jnp.exp(jnp.sum(lam_p[2:3, :] * lam_p[3:4, :], axis=-1, keepdims=True)) + lam_init)
    o = o[:, :tq] - lam * o[:, tq:]
    o = o * lax.rsqrt(jnp.mean(o * o, axis=0, keepdims=True) + EPS) * g_ref[...]
    o_ref[...] = (o * (1.0 - lam_init)).T.astype(BF16)


def _attention(qt, k, vt, lam_p, gain, lam_init, da):
    nb, heads, w, n = qt.shape
    m = k.shape[2]
    tq = _pick_tile(n, (256, 128))
    tk = _pick_tile(m, (256, 128))
    nq = n // tq
    return pl.pallas_call(
        functools.partial(_attn_kernel, tk=tk, lam_init=lam_init, da=da),
        grid=(nb, heads, nq),
        in_specs=[pl.BlockSpec((None, None, w, tq), lambda b, h, i: (b, h, 0, i)),
                  pl.BlockSpec((None, None, m, w), lambda b, h, i: (b, h, 0, 0)),
                  pl.BlockSpec((None, None, w, m), lambda b, h, i: (b, h, 0, 0)),
                  pl.BlockSpec(lam_p.shape, lambda b, h, i: (0, 0)),
                  pl.BlockSpec((w, 1), lambda b, h, i: (0, 0))],
        out_specs=pl.BlockSpec((tq, w), lambda b, h, i: (b * nq + i, h)),
        out_shape=jax.ShapeDtypeStruct((nb * n, heads * w), BF16),
        compiler_params=_params(("parallel", "parallel", "parallel")),
    )(qt, k, vt, lam_p, gain)


def _scan_constants(tb, c, reverse):
    n = tb // c
    i = np.arange(c)[:, None]
    t = np.arange(c)[None, :]
    cum = ((t >= i) if reverse else (t <= i)).astype(np.float32)
    fast = np.kron(np.eye(n, dtype=np.float32), cum)

    levels, masks = [], []
    g = c
    while g >= 2:
        h = g // 2
        mat = np.zeros((c, c), np.float32)
        for r in range(c):
            mid = (r // g) * g + h
            if not reverse:
                if r >= mid:
                    mat[r, mid:r + 1] = 1.0
                else:
                    mat[r, r + 1:mid] = 1.0
            else:
                if r < mid:
                    mat[r, r:mid] = 1.0
                else:
                    mat[r, mid:r] = 1.0
        same = (i // g) == (t // g)
        if not reverse:
            ok = same & ((i % g) >= h) & ((t % g) < h)
        else:
            ok = same & ((i % g) < h) & ((t % g) >= h)
        levels.append(mat)
        masks.append(ok.astype(np.float32))
        g = h
    masks.append(np.eye(c, dtype=np.float32))
    return (jnp.asarray(fast, BF16), jnp.asarray(np.concatenate(levels, axis=0), BF16),
            jnp.asarray(np.stack(masks), F32))


def _scan_block(q, k, v, la, fast_ref, slow_ref, mask_ref, st_ref, o_ref, *, c, reverse):
    tb = q.shape[0]
    n = tb // c
    chunks = [slice(ci * c, (ci + 1) * c) for ci in range(n)]
    ref_row = c // 2 if reverse else c // 2 - 1
    end_row = 0 if reverse else c - 1
    parts = _split3(la)
    b = _dot_exact_lhs(fast_ref[...], parts)
    b_ref = jnp.concatenate([jnp.broadcast_to(b[sl][ref_row:ref_row + 1], (c, b.shape[1])) for sl in chunks], axis=0)
    b_end = jnp.concatenate([jnp.broadcast_to(b[sl][end_row:end_row + 1], (c, b.shape[1])) for sl in chunks], axis=0)
    zq = b - b_ref
    safe = jnp.max(jnp.abs(zq)) < SAFE_EXPONENT
    q_in = (q * jnp.exp(b)).astype(BF16)
    k_up = (k * jnp.exp(b_end - b)).astype(BF16)
    v_bf = v.astype(BF16)
    row = lax.broadcasted_iota(jnp.int32, (c, c), 0)
    col = lax.broadcasted_iota(jnp.int32, (c, c), 1)
    visible = (col >= row) if reverse else (col <= row)
    n_levels = slow_ref.shape[0] // c

    def fast_scores():
        q_mid = (q * jnp.exp(zq)).astype(BF16)
        k_mid = (k * jnp.exp(-zq)).astype(BF16)
        return jnp.concatenate([jnp.where(visible, _dot_nt(q_mid[sl], k_mid[sl]), 0.0) for sl in chunks], axis=0)

    def slow_scores():
        q_bf = q.astype(BF16)
        k_bf = k.astype(BF16)
        out = []
        for sl in chunks:
            zs = _dot_exact_lhs(slow_ref[...], tuple(p[sl] for p in parts))
            a = mask_ref[n_levels] * _dot_nt(q_bf[sl], k_bf[sl])
            for lv in range(n_levels):
                e = jnp.exp(zs[lv * c:(lv + 1) * c])
                a += mask_ref[lv] * _dot_nt((q[sl] * e).astype(BF16), (k[sl] * e).astype(BF16))
            out.append(a)
        return jnp.concatenate(out, axis=0)

    a = lax.cond(safe, fast_scores, slow_scores).astype(BF16)
    st = st_ref[...]
    for ci in (range(n - 1, -1, -1) if reverse else range(n)):
        sl = chunks[ci]
        o_ref[sl, :] = _dot_nt(q_in[sl], st.astype(BF16)) + _dot(a[sl], v_bf[sl])
        st = st * jnp.exp(b_end[sl][0:1]) + _dot_tn(v_bf[sl], k_up[sl])
    st_ref[...] = st


def _scan_finish(o_ref, prev_ref, gate_ref, gn_ref, y_ref):
    o = o_ref[...] + prev_ref[...]
    o = o * lax.rsqrt(jnp.mean(o * o, axis=-1, keepdims=True) + EPS) * gn_ref[...]
    y_ref[...] = (o * _silu(gate_ref[...])).astype(BF16)


def _scan_state_io(s0_ref, sT_ref, st_ref, do_body):
    t = pl.program_id(2)

    @pl.when(t == 0)
    def _():
        st_ref[...] = s0_ref[...].T

    do_body()

    @pl.when(t == pl.num_programs(2) - 1)
    def _():
        sT_ref[...] = st_ref[...].T


def _hgrn_kernel(*refs, c, reverse, last_pass):
    if last_pass:
        (q_ref, f_ref, i_ref, lb_ref, s0_ref, fast_ref, slow_ref, mask_ref, prev_ref, gate_ref, gn_ref,
         y_ref, sT_ref, st_ref, o_scr) = refs
    else:
        (q_ref, f_ref, i_ref, lb_ref, s0_ref, fast_ref, slow_ref, mask_ref,
         y_ref, sT_ref, st_ref) = refs
        o_scr = y_ref

    def body():
        lb = lb_ref[...]
        f = lb + (1.0 - lb) * _sigmoid(f_ref[...])
        _scan_block(_silu(q_ref[...]), 1.0 - f, i_ref[...], jnp.log(f),
                    fast_ref, slow_ref, mask_ref, st_ref, o_scr, c=c, reverse=reverse)
        if last_pass:
            _scan_finish(o_scr, prev_ref, gate_ref, gn_ref, y_ref)

    _scan_state_io(s0_ref, sT_ref, st_ref, body)


def _gla_kernel(*refs, c, reverse, last_pass, rank, q_scale):
    if last_pass:
        (q_ref, k_ref, v_ref, r_ref, wd_ref, bd_ref, s0_ref, fast_ref, slow_ref, mask_ref,
         prev_ref, gate_ref, gn_ref, y_ref, sT_ref, st_ref, o_scr) = refs
    else:
        (q_ref, k_ref, v_ref, r_ref, wd_ref, bd_ref, s0_ref, fast_ref, slow_ref, mask_ref,
         y_ref, sT_ref, st_ref) = refs
        o_scr = y_ref

    def body():
        lo = rank if reverse else 0
        r = r_ref[...][:, lo:lo + rank].astype(BF16)
        logit = _dot(r, wd_ref[...].astype(BF16)) + bd_ref[...]
        la = (jnp.minimum(logit, 0.0) - jnp.log(1.0 + jnp.exp(-jnp.abs(logit)))) / GLA_TAU
        _scan_block(q_ref[...] * q_scale, k_ref[...], v_ref[...], la,
                    fast_ref, slow_ref, mask_ref, st_ref, o_scr, c=c, reverse=reverse)
        if last_pass:
            _scan_finish(o_scr, prev_ref, gate_ref, gn_ref, y_ref)

    _scan_state_io(s0_ref, sT_ref, st_ref, body)


def _scan_call(kernel, proj, col_specs, extra, s0, prev, gate_col, gn, nb, t, heads, dk, dv, row0,
               reverse, **kw):
    tb = _pick_tile(t, (SCAN_BLOCK, 128))
    c = min(SCAN_CHUNK, tb)
    nblk = t // tb
    rb0 = row0 // tb
    last_pass = prev is not None
    fast, slow, masks = _scan_constants(tb, c, reverse)

    def tblk(i):
        return (nblk - 1 - i) if reverse else i

    def col(off, width, per_head=True):
        step = 1 if per_head else 0
        return pl.BlockSpec((tb, width), lambda b, h, i: (rb0 + b * nblk + tblk(i), off // width + step * h))

    in_specs = [col(*spec) for spec in col_specs]
    args = [proj] * len(col_specs)
    for e in extra:
        in_specs.append(pl.BlockSpec((None,) + e.shape[1:], lambda b, h, i: (h, 0, 0)))
        args.append(e)
    in_specs.append(pl.BlockSpec((None, None, dk, dv), lambda b, h, i: (b, h, 0, 0)))
    args.append(s0)
    for cst in (fast, slow, masks):
        in_specs.append(pl.BlockSpec(cst.shape, lambda b, h, i, nd=cst.ndim: (0,) * nd))
        args.append(cst)
    scratch = [pltpu.VMEM((dv, dk), F32)]
    if last_pass:
        in_specs += [pl.BlockSpec((tb, dv), lambda b, h, i: (b * nblk + tblk(i), h)),
                     col(gate_col, dv),
                     pl.BlockSpec((1, dv), lambda b, h, i: (0, 0))]
        args += [prev, proj, gn]
        scratch.append(pltpu.VMEM((tb, dv), F32))
    y_dtype = BF16 if last_pass else F32
    return pl.pallas_call(
        functools.partial(kernel, c=c, reverse=reverse, last_pass=last_pass, **kw),
        grid=(nb, heads, nblk),
        in_specs=in_specs,
        out_specs=[pl.BlockSpec((tb, dv), lambda b, h, i: (b * nblk + tblk(i), h)),
                   pl.BlockSpec((None, None, dk, dv), lambda b, h, i: (b, h, 0, 0))],
        out_shape=[jax.ShapeDtypeStruct((nb * t, heads * dv), y_dtype),
                   jax.ShapeDtypeStruct((nb, heads, dk, dv), F32)],
        scratch_shapes=scratch,
        compiler_params=_params(("parallel", "parallel", "arbitrary")),
    )(*args)


class _Geometry:
    pass


def _rope_tables(n_tok, da):
    ax = da // 2
    rows = n_tok // GRID_W
    r = jnp.broadcast_to(jnp.arange(rows, dtype=F32)[:, None], (rows, GRID_W)).reshape(-1)
    col = jnp.broadcast_to(jnp.arange(GRID_W, dtype=F32)[None, :], (rows, GRID_W)).reshape(-1)
    inv = ROPE_BASE ** (-jnp.arange(0, ax, 2, dtype=F32) / ax)
    ar, ac = r[:, None] * inv, col[:, None] * inv
    ang = jnp.concatenate([ar, ar, ac, ac], axis=-1)
    cos, sin = jnp.cos(ang), jnp.sin(ang)
    q = da // 4
    sign = jnp.where((jnp.arange(da) % (2 * q)) < q, -1.0, 1.0).astype(F32)
    return jnp.tile(cos, (1, 2)), jnp.tile(sin * sign, (1, 2))


def kernel(x_prompt, x_sample, c, cache_attn_k, cache_attn_v, state_hgrn, state_gla, c_ctx, ada_w, ada_b, norm_w, ffn_w_gate, ffn_w_up, ffn_w_down, w_in, attn_qk_norm, attn_lambda, attn_out_norm, hgrn_lb_logits, hgrn_out_norm, gla_w_decay, gla_b_decay, gla_out_norm, w_branch, w_out):
    batch, seq, d = x_prompt.shape
    dec_batch, dec_seq, _ = x_sample.shape
    depth = ada_w.shape[0]
    ha, da = cache_attn_k.shape[2], cache_attn_k.shape[5]
    dva = 2 * da
    hh, hdk, hdv = state_hgrn.shape[3:]
    hg, gdk, gdv = state_gla.shape[3:]
    rank = gla_w_decay.shape[2]
    sizes = (ha * 2 * da, ha * 2 * da, ha * dva, hh * hdk, hh * hdk, hh * hdk, hh * hdv, hh * hdv,
             hg * gdk, hg * gdk, hg * gdv, hg * gdv, 2 * rank, 3 * d)
    assert sum(sizes) == w_in.shape[2] and hdk == LANES and hdv == LANES and gdk == LANES and dva == LANES

    geo = _Geometry()
    geo.n_ctx_rows = batch * seq
    geo.dec_seq = dec_seq
    geo.tile_rows = math.gcd(geo.n_ctx_rows, dec_seq)
    n_ctx = geo.n_ctx_rows

    src = np.concatenate([[0], np.cumsum(sizes)])
    n_main = int(src[12])
    glr_pad = 2 * LANES
    names = ('aq', 'ak', 'av', 'hq', 'hf_f', 'hf_b', 'hi', 'hg', 'gq', 'gk', 'gv', 'gg')
    off = {nm: 3 * d + int(src[i]) for i, nm in enumerate(names)}
    off['glr'] = 3 * d + n_main

    def pack_w_in(w):
        return jnp.concatenate([w[:, src[13]:], w[:, :n_main], w[:, src[12]:src[13]],
                                jnp.zeros((d, glr_pad - 2 * rank), w.dtype)], axis=1).astype(BF16)

    n_cond = 1 + dec_batch
    cond = jnp.concatenate([c_ctx[None], c, jnp.zeros((-n_cond % 8, d), F32)], axis=0)
    mod = _adaln(cond, ada_w, ada_b).reshape(depth, cond.shape[0], 9, d)

    lb_w = jax.nn.softmax(hgrn_lb_logits.astype(F32), axis=0)
    lower_bounds = jnp.cumsum(lb_w, axis=0) - lb_w[0]
    rope_tabs = _rope_tables(dec_seq, da)

    x = jnp.concatenate([x_prompt.reshape(n_ctx, d), x_sample.reshape(dec_batch * dec_seq, d)], axis=0)
    zeros_h = jnp.zeros((batch, hh, hdk, hdv), F32)
    zeros_g = jnp.zeros((batch, hg, gdk, gdv), F32)
    ks_l, vs_l, sh_l, sg_l = [], [], [], []

    for l in range(depth):
        m_l = mod[l]
        nw = norm_w[l]
        x = _ffn(x, m_l, nw[0:1], ffn_w_gate[l, 0].astype(BF16), ffn_w_up[l, 0].astype(BF16),
                 ffn_w_down[l, 0].astype(BF16), 0, geo)
        proj = _inproj(x, m_l, nw[1:2], pack_w_in(w_in[l]), 3, geo)

        lam_init = 0.8 - 0.6 * math.exp(-0.3 * l)
        gains = jnp.tile(attn_qk_norm[l], (1, 2))
        cols = (off['aq'], off['ak'], off['av'])
        gn_a = attn_out_norm[l][:, None]
        qp, kp, vp, k32, v32 = _attn_prep(proj, cols, gains, None, batch, seq, ha, da, 0, True)
        ks_l.append(k32.reshape(batch, ha, seq, 2, da))
        vs_l.append(v32)
        qs, ksm, vsm = _attn_prep(proj, cols, gains, rope_tabs, dec_batch, dec_seq, ha, da, n_ctx, False)
        k_all = jnp.concatenate([cache_attn_k[:, l].reshape(dec_batch, ha, -1, 2 * da).astype(BF16), ksm], axis=2)
        v_all = jnp.concatenate([jnp.swapaxes(cache_attn_v[:, l], 2, 3).astype(BF16), vsm], axis=3)
        br_a = jnp.concatenate([_attention(qp, kp, vp, attn_lambda[l], gn_a, lam_init, da),
                                _attention(qs, k_all, v_all, attn_lambda[l], gn_a, lam_init, da)], axis=0)

        lb = lower_bounds[l].reshape(2, hh, 1, hdk)
        gn_h = hgrn_out_norm[l][None]

        def hgrn(nb, t, row0, s0):
            specs_b = ((off['hq'], hdk), (off['hf_b'], hdk), (off['hi'], hdv))
            specs_f = ((off['hq'], hdk), (off['hf_f'], hdk), (off['hi'], hdv))
            o_b, s_b = _scan_call(_hgrn_kernel, proj, specs_b, [lb[1]], s0[1], None, None, None,
                                  nb, t, hh, hdk, hdv, row0, True)
            y, s_f = _scan_call(_hgrn_kernel, proj, specs_f, [lb[0]], s0[0], o_b, off['hg'], gn_h,
                                nb, t, hh, hdk, hdv, row0, False)
            return y, jnp.stack([s_f, s_b], axis=1)

        y_p, s_h = hgrn(batch, seq, 0, (zeros_h, zeros_h))
        y_s, _ = hgrn(dec_batch, dec_seq, n_ctx, (state_hgrn[:, l, 0], state_hgrn[:, l, 1]))
        br_h = jnp.concatenate([y_p, y_s], axis=0)
        sh_l.append(s_h)

        wd = gla_w_decay[l].reshape(2, rank, hg, gdk).transpose(0, 2, 1, 3)
        bd = gla_b_decay[l].reshape(2, hg, 1, gdk)
        gn_g = gla_out_norm[l][None]

        def gla(nb, t, row0, s0):
            specs = ((off['gq'], gdk), (off['gk'], gdk), (off['gv'], gdv), (off['glr'], LANES, False))
            kw = dict(rank=rank, q_scale=gdk ** -0.5)
            o_b, s_b = _scan_call(_gla_kernel, proj, specs, [wd[1], bd[1]], s0[1], None, None, None,
                                  nb, t, hg, gdk, gdv, row0, True, **kw)
            y, s_f = _scan_call(_gla_kernel, proj, specs, [wd[0], bd[0]], s0[0], o_b, off['gg'], gn_g,
                                nb, t, hg, gdk, gdv, row0, False, **kw)
            return y, jnp.stack([s_f, s_b], axis=1)

        y_p, s_g = gla(batch, seq, 0, (zeros_g, zeros_g))
        y_s, _ = gla(dec_batch, dec_seq, n_ctx, (state_gla[:, l, 0], state_gla[:, l, 1]))
        br_g = jnp.concatenate([y_p, y_s], axis=0)
        sg_l.append(s_g)

        merged = _merge(br_a, br_h, br_g, proj, w_branch[l].astype(BF16), geo)
        x = _outproj(merged, w_out[l].astype(BF16), x, m_l, 5, geo)
        x = _ffn(x, m_l, nw[2:3], ffn_w_gate[l, 1].astype(BF16), ffn_w_up[l, 1].astype(BF16),
                 ffn_w_down[l, 1].astype(BF16), 6, geo)

    y_prompt = x[:n_ctx].reshape(batch, seq, d)
    y_sample = x[n_ctx:].reshape(dec_batch, dec_seq, d)
    return (y_prompt, y_sample, jnp.stack(ks_l, axis=1), jnp.stack(vs_l, axis=1),
            jnp.stack(sh_l, axis=1), jnp.stack(sg_l, axis=1))
```

```python
import functools
import math

import jax
import jax.numpy as jnp
import numpy as np
from jax import lax
from jax.experimental import pallas as pl
from jax.experimental.pallas import tpu as pltpu

F32 = jnp.float32
BF16 = jnp.bfloat16

GRID_W = 64
GLA_TAU = 16.0
ROPE_BASE = 10000.0
EPS = 1e-6
LANES = 128
SCAN_CHUNK = 64
SCAN_BLOCK = 256
SCAN_HEADS = 4
ATTN_LOOKAHEAD = 2
SAFE_EXPONENT = 80.0
VMEM_LIMIT = 56 * 1024 * 1024


def _dot(a, b):
    return jnp.dot(a, b, preferred_element_type=F32)


def _dot_nt(a, b):
    return lax.dot_general(a, b, (((1,), (1,)), ((), ())), preferred_element_type=F32)


def _dot_tn(a, b):
    return lax.dot_general(a, b, (((0,), (0,)), ((), ())), preferred_element_type=F32)


def _sigmoid(x):
    return 1.0 / (1.0 + jnp.exp(-x))


def _silu(x):
    return x * _sigmoid(x)


def _split3(x):
    hi = x.astype(BF16)
    r1 = x - hi.astype(F32)
    mid = r1.astype(BF16)
    lo = (r1 - mid.astype(F32)).astype(BF16)
    return hi, mid, lo


def _dot_exact_lhs(m, parts):
    return _dot(m, parts[0]) + _dot(m, parts[1]) + _dot(m, parts[2])


def _pick_tile(n, candidates):
    for c in candidates:
        if n % c == 0:
            return c
    return n


def _params(sem):
    return pltpu.CompilerParams(dimension_semantics=sem, vmem_limit_bytes=VMEM_LIMIT)


def _adaln_kernel(c_ref, w_ref, b_ref, o_ref):
    s = _silu(c_ref[...]).astype(BF16)
    o_ref[...] = _dot(s, w_ref[...].astype(BF16)) + b_ref[...]


def _adaln(cond, ada_w, ada_b):
    depth, d, n = ada_w.shape
    rows = cond.shape[0]
    tn = _pick_tile(n, (1024, 512, 256, 128))
    return pl.pallas_call(
        _adaln_kernel,
        grid=(depth, n // tn),
        in_specs=[pl.BlockSpec((rows, d), lambda l, j: (0, 0)),
                  pl.BlockSpec((None, d, tn), lambda l, j: (l, 0, j)),
                  pl.BlockSpec((None, 1, tn), lambda l, j: (l, 0, j))],
        out_specs=pl.BlockSpec((None, rows, tn), lambda l, j: (l, 0, j)),
        out_shape=jax.ShapeDtypeStruct((depth, rows, n), F32),
        compiler_params=_params(("parallel", "parallel")),
    )(cond, ada_w, ada_b.reshape(depth, 1, n))


def _modulated(x, nw, shift, scale):
    y = x * lax.rsqrt(jnp.mean(x * x, axis=-1, keepdims=True) + EPS) * nw
    return y * (1.0 + scale) + shift


def _ffn_kernel(x_ref, m_ref, nw_ref, wg_ref, wu_ref, wd_ref, o_ref, h_ref, *, base):
    j = pl.program_id(1)

    @pl.when(j == 0)
    def _():
        h = _modulated(x_ref[...], nw_ref[...], m_ref[base:base + 1, :], m_ref[base + 1:base + 2, :])
        h_ref[...] = h.astype(BF16)
        o_ref[...] = jnp.zeros_like(o_ref)

    h = h_ref[...]
    g = _dot(h, wg_ref[...])
    u = _dot(h, wu_ref[...])
    o_ref[...] += _dot((_silu(g) * u).astype(BF16), wd_ref[...])

    @pl.when(j == pl.num_programs(1) - 1)
    def _():
        o_ref[...] = x_ref[...] + 0.5 * m_ref[base + 2:base + 3, :] * o_ref[...]


def _cond_index(i, tm, n_ctx_rows, seq_rows):
    r = i * tm
    return jnp.where(r < n_ctx_rows, 0, (r - n_ctx_rows) // seq_rows + 1)


def _ffn(x, mod, nw, wg, wu, wd, base, geo):
    nt, d = x.shape
    f = wg.shape[1]
    tm = _pick_tile(geo.tile_rows, (512, 256, 128))
    tf = _pick_tile(f, (512, 256, 128))
    cidx = functools.partial(_cond_index, tm=tm, n_ctx_rows=geo.n_ctx_rows, seq_rows=geo.dec_seq)
    return pl.pallas_call(
        functools.partial(_ffn_kernel, base=base),
        grid=(nt // tm, f // tf),
        in_specs=[pl.BlockSpec((tm, d), lambda i, j: (i, 0)),
                  pl.BlockSpec((None, 9, d), lambda i, j: (cidx(i), 0, 0)),
                  pl.BlockSpec((1, d), lambda i, j: (0, 0)),
                  pl.BlockSpec((d, tf), lambda i, j: (0, j)),
                  pl.BlockSpec((d, tf), lambda i, j: (0, j)),
                  pl.BlockSpec((tf, d), lambda i, j: (j, 0))],
        out_specs=pl.BlockSpec((tm, d), lambda i, j: (i, 0)),
        out_shape=jax.ShapeDtypeStruct((nt, d), F32),
        scratch_shapes=[pltpu.VMEM((tm, d), BF16)],
        compiler_params=_params(("parallel", "arbitrary")),
    )(x, mod, nw, wg, wu, wd)


def _inproj_kernel(x_ref, m_ref, nw_ref, w_ref, o_ref, h_ref, *, base):
    @pl.when(pl.program_id(1) == 0)
    def _():
        h = _modulated(x_ref[...], nw_ref[...], m_ref[base:base + 1, :], m_ref[base + 1:base + 2, :])
        h_ref[...] = h.astype(BF16)

    o_ref[...] = _dot(h_ref[...], w_ref[...])


def _inproj(x, mod, nw, w, base, geo):
    nt, d = x.shape
    n = w.shape[1]
    tm = _pick_tile(geo.tile_rows, (1024, 512, 256, 128))
    tn = _pick_tile(n, (768, 512, 256, 128))
    cidx = functools.partial(_cond_index, tm=tm, n_ctx_rows=geo.n_ctx_rows, seq_rows=geo.dec_seq)
    return pl.pallas_call(
        functools.partial(_inproj_kernel, base=base),
        grid=(nt // tm, n // tn),
        in_specs=[pl.BlockSpec((tm, d), lambda i, j: (i, 0)),
                  pl.BlockSpec((None, 9, d), lambda i, j: (cidx(i), 0, 0)),
                  pl.BlockSpec((1, d), lambda i, j: (0, 0)),
                  pl.BlockSpec((d, tn), lambda i, j: (0, j))],
        out_specs=pl.BlockSpec((tm, tn), lambda i, j: (i, j)),
        out_shape=jax.ShapeDtypeStruct((nt, n), F32),
        scratch_shapes=[pltpu.VMEM((tm, d), BF16)],
        compiler_params=_params(("parallel", "arbitrary")),
    )(x, mod, nw, w)


def _merge_kernel(ba_ref, bh_ref, bg_ref, ga_ref, gh_ref, gg_ref, wb_ref, o_ref):
    acc = _sigmoid(ga_ref[...]) * _dot(ba_ref[...], wb_ref[0])
    acc += _sigmoid(gh_ref[...]) * _dot(bh_ref[...], wb_ref[1])
    acc += _sigmoid(gg_ref[...]) * _dot(bg_ref[...], wb_ref[2])
    o_ref[...] = acc.astype(BF16)


def _merge(br_a, br_h, br_g, proj, wb, geo):
    nt, bw = br_a.shape
    d = wb.shape[2]
    tm = _pick_tile(geo.tile_rows, (512, 256, 128))
    tn = _pick_tile(d, (512, 256, 128))
    nj = d // tn
    bspec = pl.BlockSpec((tm, bw), lambda i, j: (i, 0))
    return pl.pallas_call(
        _merge_kernel,
        grid=(nt // tm, nj),
        in_specs=[bspec, bspec, bspec,
                  pl.BlockSpec((tm, tn), lambda i, j: (i, j)),
                  pl.BlockSpec((tm, tn), lambda i, j: (i, j + nj)),
                  pl.BlockSpec((tm, tn), lambda i, j: (i, j + 2 * nj)),
                  pl.BlockSpec((3, bw, tn), lambda i, j: (0, 0, j))],
        out_specs=pl.BlockSpec((tm, tn), lambda i, j: (i, j)),
        out_shape=jax.ShapeDtypeStruct((nt, d), BF16),
        compiler_params=_params(("parallel", "parallel")),
    )(br_a, br_h, br_g, proj, proj, proj, wb)


def _outproj_kernel(a_ref, w_ref, x_ref, m_ref, o_ref, *, row):
    o_ref[...] = x_ref[...] + m_ref[row:row + 1, :] * _dot(a_ref[...], w_ref[...])


def _outproj(a, w, x, mod, row, geo):
    nt, d = x.shape
    k = a.shape[1]
    tm = _pick_tile(geo.tile_rows, (512, 256, 128))
    tn = _pick_tile(d, (512, 256, 128))
    cidx = functools.partial(_cond_index, tm=tm, n_ctx_rows=geo.n_ctx_rows, seq_rows=geo.dec_seq)
    return pl.pallas_call(
        functools.partial(_outproj_kernel, row=row),
        grid=(nt // tm, d // tn),
        in_specs=[pl.BlockSpec((tm, k), lambda i, j: (i, 0)),
                  pl.BlockSpec((k, tn), lambda i, j: (0, j)),
                  pl.BlockSpec((tm, tn), lambda i, j: (i, j)),
                  pl.BlockSpec((None, 9, tn), lambda i, j: (cidx(i), 0, j))],
        out_specs=pl.BlockSpec((tm, tn), lambda i, j: (i, j)),
        out_shape=jax.ShapeDtypeStruct((nt, d), F32),
        compiler_params=_params(("parallel", "parallel")),
    )(a, w, x, mod)


def _half_mean_matrix(da):
    m = np.kron(np.eye(2), np.ones((da, da))) / da
    return jnp.asarray(m, BF16)


def _qk_normed(x, gmat, gain):
    sq = x * x
    hi = sq.astype(BF16)
    lo = (sq - hi.astype(F32)).astype(BF16)
    msq = _dot(hi, gmat) + _dot(lo, gmat)
    return x * lax.rsqrt(msq + EPS) * gain


def _rotated(x, cos, sin_signed, da):
    q = da // 4
    width = x.shape[-1]
    lane = lax.broadcasted_iota(jnp.int32, x.shape, 1)
    first = (lane % (2 * q)) < q
    partner = jnp.where(first, pltpu.roll(x, width - q, axis=1), pltpu.roll(x, q, axis=1))
    return x * cos + partner * sin_signed


def _attn_prep_kernel(*refs, rope, emit_f32, da):
    if rope:
        q_ref, k_ref, v_ref, gm_ref, gn_ref, cos_ref, sin_ref = refs[:7]
        outs = refs[7:]
    else:
        q_ref, k_ref, v_ref, gm_ref, gn_ref = refs[:5]
        outs = refs[5:]
    gm = gm_ref[...]
    q = _qk_normed(q_ref[...], gm, gn_ref[0:1, :])
    k = _qk_normed(k_ref[...], gm, gn_ref[1:2, :])
    if rope:
        q = _rotated(q, cos_ref[...], sin_ref[...], da)
        k = _rotated(k, cos_ref[...], sin_ref[...], da)
    v = v_ref[...]
    outs[0][...] = (q * (math.log2(math.e) * da ** -0.5)).T.astype(BF16)
    outs[1][...] = k.astype(BF16)
    outs[2][...] = v.T.astype(BF16)
    if emit_f32:
        outs[3][...] = k
        outs[4][...] = v


def _attn_prep(proj, cols, gains, rope_tabs, nb, t, heads, da, row0, emit_f32):
    w = 2 * da
    tt = _pick_tile(t, (512, 256, 128))
    nblk = t // tt
    rb0 = row0 // tt
    cq, ck, cv = (c // w for c in cols)
    rope = rope_tabs is not None

    def col(c0):
        return pl.BlockSpec((tt, w), lambda b, h, i: (rb0 + b * nblk + i, c0 + h))

    in_specs = [col(cq), col(ck), col(cv),
                pl.BlockSpec((w, w), lambda b, h, i: (0, 0)),
                pl.BlockSpec((2, w), lambda b, h, i: (0, 0))]
    args = [proj, proj, proj, _half_mean_matrix(da), gains]
    if rope:
        in_specs += [pl.BlockSpec((tt, w), lambda b, h, i: (i, 0))] * 2
        args += list(rope_tabs)
    rows = pl.BlockSpec((None, None, tt, w), lambda b, h, i: (b, h, i, 0))
    cols_t = pl.BlockSpec((None, None, w, tt), lambda b, h, i: (b, h, 0, i))
    out_specs = [cols_t, rows, cols_t]
    out_shape = [jax.ShapeDtypeStruct((nb, heads, w, t), BF16),
                 jax.ShapeDtypeStruct((nb, heads, t, w), BF16),
                 jax.ShapeDtypeStruct((nb, heads, w, t), BF16)]
    if emit_f32:
        out_specs += [rows, rows]
        out_shape += [jax.ShapeDtypeStruct((nb, heads, t, w), F32)] * 2
    return pl.pallas_call(
        functools.partial(_attn_prep_kernel, rope=rope, emit_f32=emit_f32, da=da),
        grid=(nb, heads, nblk),
        in_specs=in_specs,
        out_specs=out_specs,
        out_shape=out_shape,
        compiler_params=_params(("parallel", "parallel", "parallel")),
    )(*args)


def _attn_kernel(qt_ref, k_ref, vt_ref, lam_ref, g_ref, o_ref, *, tk, lam_init, da):
    nk = k_ref.shape[0] // tk
    qt = qt_ref[...]
    feat = lax.broadcasted_iota(jnp.int32, qt.shape, 0)
    zero = jnp.zeros_like(qt)

    tq = qt.shape[1]
    q2t = jnp.concatenate([jnp.where(feat < da, qt, zero), jnp.where(feat >= da, qt, zero)], axis=1)

    def scores(i):
        return _dot(k_ref[i * tk:(i + 1) * tk, :], q2t)

    ahead = [scores(i) for i in range(min(ATTN_LOOKAHEAD, nk))]
    m = l = acc = None
    for i in range(nk):
        s = ahead.pop(0)
        if i + ATTN_LOOKAHEAD < nk:
            ahead.append(scores(i + ATTN_LOOKAHEAD))
        m_tile = jnp.max(s, axis=0, keepdims=True)
        m_new = m_tile if i == 0 else jnp.maximum(m, m_tile)
        p = jnp.exp2(s - m_new)
        pv = _dot(vt_ref[:, i * tk:(i + 1) * tk], p.astype(BF16))
        if i == 0:
            l, acc = jnp.sum(p, axis=0, keepdims=True), pv
        else:
            alpha = jnp.exp2(m - m_new)
            l = alpha * l + jnp.sum(p, axis=0, keepdims=True)
            acc = alpha * acc + pv
        m = m_new
    o = acc / l
    lam_p = lam_ref[...]
    lam = (jnp.exp(jnp.sum(lam_p[0:1, :] * lam_p[1:2, :], axis=-1, keepdims=True))
           - jnp.exp(jnp.sum(lam_p[2:3, :] * lam_p[3:4, :], axis=-1, keepdims=True)) + lam_init)
    o = o[:, :tq] - lam * o[:, tq:]
    o = o * lax.rsqrt(jnp.mean(o * o, axis=0, keepdims=True) + EPS) * g_ref[...]
    o_ref[...] = (o * (1.0 - lam_init)).T.astype(BF16)


def _attention(qt, k, vt, lam_p, gain, lam_init, da):
    nb, heads, w, n = qt.shape
    m = k.shape[2]
    tq = _pick_tile(n, (256, 128))
    tk = _pick_tile(m, (256, 128))
    nq = n // tq
    return pl.pallas_call(
        functools.partial(_attn_kernel, tk=tk, lam_init=lam_init, da=da),
        grid=(nb, heads, nq),
        in_specs=[pl.BlockSpec((None, None, w, tq), lambda b, h, i: (b, h, 0, i)),
                  pl.BlockSpec((None, None, m, w), lambda b, h, i: (b, h, 0, 0)),
                  pl.BlockSpec((None, None, w, m), lambda b, h, i: (b, h, 0, 0)),
                  pl.BlockSpec(lam_p.shape, lambda b, h, i: (0, 0)),
                  pl.BlockSpec((w, 1), lambda b, h, i: (0, 0))],
        out_specs=pl.BlockSpec((tq, w), lambda b, h, i: (b * nq + i, h)),
        out_shape=jax.ShapeDtypeStruct((nb * n, heads * w), BF16),
        compiler_params=_params(("parallel", "parallel", "parallel")),
    )(qt, k, vt, lam_p, gain)


def _scan_constants(c, reverse):
    i = np.arange(c)[:, None]
    t = np.arange(c)[None, :]

    levels, masks = [], []
    g = c
    while g >= 2:
        h = g // 2
        mat = np.zeros((c, c), np.float32)
        for r in range(c):
            mid = (r // g) * g + h
            if not reverse:
                if r >= mid:
                    mat[r, mid:r + 1] = 1.0
                else:
                    mat[r, r + 1:mid] = 1.0
            else:
                if r < mid:
                    mat[r, r:mid] = 1.0
                else:
                    mat[r, mid:r] = 1.0
        same = (i // g) == (t // g)
        if not reverse:
            ok = same & ((i % g) >= h) & ((t % g) < h)
        else:
            ok = same & ((i % g) < h) & ((t % g) >= h)
        levels.append(mat)
        masks.append(ok.astype(np.float32))
        g = h
    masks.append(np.eye(c, dtype=np.float32))
    return jnp.asarray(np.concatenate(levels, axis=0), BF16), jnp.asarray(np.stack(masks), F32)


def _chunk_cumsum(x, c, reverse):
    tb = x.shape[0]
    pos = lax.broadcasted_iota(jnp.int32, x.shape, 0) % c
    s = 1
    while s < c:
        if reverse:
            x = x + jnp.where(pos < c - s, pltpu.roll(x, tb - s, axis=0), 0.0)
        else:
            x = x + jnp.where(pos >= s, pltpu.roll(x, s, axis=0), 0.0)
        s *= 2
    return x


def _scan_block(q, k, v, la, slow_ref, mask_ref, st_ref, u_ref, o_ref, *, c, reverse):
    tb = q.shape[0]
    n = tb // c
    hb, dv, dk = st_ref.shape
    chunks = [slice(ci * c, (ci + 1) * c) for ci in range(n)]
    kcols = [slice(j * dk, (j + 1) * dk) for j in range(hb)]
    vcols = [slice(j * dv, (j + 1) * dv) for j in range(hb)]
    ref_row = c // 2 if reverse else c // 2 - 1
    end_row = 0 if reverse else c - 1
    b = _chunk_cumsum(la, c, reverse)
    b_ref = jnp.concatenate([jnp.broadcast_to(b[sl][ref_row:ref_row + 1], (c, b.shape[1])) for sl in chunks], axis=0)
    b_end = jnp.concatenate([jnp.broadcast_to(b[sl][end_row:end_row + 1], (c, b.shape[1])) for sl in chunks], axis=0)
    zq = b - b_ref
    safe = jnp.max(jnp.abs(zq)) < SAFE_EXPONENT
    q_in = (q * jnp.exp(b)).astype(BF16)
    k_up = (k * jnp.exp(b_end - b)).astype(BF16)
    v_bf = v.astype(BF16)
    row = lax.broadcasted_iota(jnp.int32, (c, c), 0)
    col = lax.broadcasted_iota(jnp.int32, (c, c), 1)
    visible = (col >= row) if reverse else (col <= row)
    n_levels = slow_ref.shape[0] // c

    def fast_scores():
        q_mid = (q * jnp.exp(zq)).astype(BF16)
        k_mid = (k * jnp.exp(-zq)).astype(BF16)
        return jnp.concatenate([jnp.where(visible, _dot_nt(q_mid[sl, kc], k_mid[sl, kc]), 0.0)
                                for kc in kcols for sl in chunks], axis=0)

    def slow_scores():
        q_bf = q.astype(BF16)
        k_bf = k.astype(BF16)
        parts = _split3(la)
        out = []
        for kc in kcols:
            for sl in chunks:
                zs = _dot_exact_lhs(slow_ref[...], tuple(p[sl, kc] for p in parts))
                a = mask_ref[n_levels] * _dot_nt(q_bf[sl, kc], k_bf[sl, kc])
                for lv in range(n_levels):
                    e = jnp.exp(zs[lv * c:(lv + 1) * c])
                    a += mask_ref[lv] * _dot_nt((q[sl, kc] * e).astype(BF16), (k[sl, kc] * e).astype(BF16))
                out.append(a)
        return jnp.concatenate(out, axis=0)

    a = lax.cond(safe, fast_scores, slow_scores).astype(BF16)
    for j in range(hb):
        for ci, sl in enumerate(chunks):
            u_ref[j * n + ci] = _dot_tn(v_bf[sl, vcols[j]], k_up[sl, kcols[j]])
    entry = {}
    for j in range(hb):
        st = st_ref[j]
        for ci in (range(n - 1, -1, -1) if reverse else range(n)):
            entry[j, ci] = st.astype(BF16)
            st = st * jnp.exp(b_end[chunks[ci], kcols[j]][0:1]) + u_ref[j * n + ci]
        st_ref[j] = st
    for j in range(hb):
        for ci, sl in enumerate(chunks):
            rows = slice(j * tb + ci * c, j * tb + (ci + 1) * c)
            o_ref[sl, vcols[j]] = _dot_nt(q_in[sl, kcols[j]], entry[j, ci]) + _dot(a[rows], v_bf[sl, vcols[j]])


def _scan_finish(o_ref, prev_ref, gate_ref, gn_ref, y_ref):
    dv = gn_ref.shape[1]
    o = o_ref[...] + prev_ref[...]
    gate = _silu(gate_ref[...])
    for j in range(o.shape[1] // dv):
        cols = slice(j * dv, (j + 1) * dv)
        oj = o[:, cols]
        oj = oj * lax.rsqrt(jnp.mean(oj * oj, axis=-1, keepdims=True) + EPS) * gn_ref[...]
        y_ref[:, cols] = (oj * gate[:, cols]).astype(BF16)


def _scan_state_io(s0_ref, sT_ref, st_ref, do_body):
    t = pl.program_id(2)
    hb = st_ref.shape[0]

    @pl.when(t == 0)
    def _():
        for j in range(hb):
            st_ref[j] = s0_ref[j].T

    do_body()

    @pl.when(t == pl.num_programs(2) - 1)
    def _():
        for j in range(hb):
            sT_ref[j] = st_ref[j].T


def _hgrn_kernel(*refs, c, reverse, last_pass):
    if last_pass:
        (q_ref, f_ref, i_ref, lb_ref, s0_ref, slow_ref, mask_ref, prev_ref, gate_ref, gn_ref,
         y_ref, sT_ref, st_ref, u_ref, o_scr) = refs
    else:
        (q_ref, f_ref, i_ref, lb_ref, s0_ref, slow_ref, mask_ref,
         y_ref, sT_ref, st_ref, u_ref) = refs
        o_scr = y_ref

    def body():
        lb = lb_ref[...]
        f = lb + (1.0 - lb) * _sigmoid(f_ref[...])
        _scan_block(_silu(q_ref[...]), 1.0 - f, i_ref[...], jnp.log(f),
                    slow_ref, mask_ref, st_ref, u_ref, o_scr, c=c, reverse=reverse)
        if last_pass:
            _scan_finish(o_scr, prev_ref, gate_ref, gn_ref, y_ref)

    _scan_state_io(s0_ref, sT_ref, st_ref, body)


def _gla_kernel(*refs, c, reverse, last_pass, rank, q_scale):
    if last_pass:
        (q_ref, k_ref, v_ref, r_ref, wd_ref, bd_ref, s0_ref, slow_ref, mask_ref,
         prev_ref, gate_ref, gn_ref, y_ref, sT_ref, st_ref, u_ref, o_scr) = refs
    else:
        (q_ref, k_ref, v_ref, r_ref, wd_ref, bd_ref, s0_ref, slow_ref, mask_ref,
         y_ref, sT_ref, st_ref, u_ref) = refs
        o_scr = y_ref

    def body():
        lo = rank if reverse else 0
        r = r_ref[...][:, lo:lo + rank].astype(BF16)
        logit = _dot(r, wd_ref[...].astype(BF16)) + bd_ref[...]
        la = (jnp.minimum(logit, 0.0) - jnp.log(1.0 + jnp.exp(-jnp.abs(logit)))) / GLA_TAU
        _scan_block(q_ref[...] * q_scale, k_ref[...], v_ref[...], la,
                    slow_ref, mask_ref, st_ref, u_ref, o_scr, c=c, reverse=reverse)
        if last_pass:
            _scan_finish(o_scr, prev_ref, gate_ref, gn_ref, y_ref)

    _scan_state_io(s0_ref, sT_ref, st_ref, body)


def _scan_call(kernel, proj, col_specs, extra, s0, prev, gate_col, gn, nb, t, heads, dk, dv, row0,
               reverse, **kw):
    tb = _pick_tile(t, (SCAN_BLOCK, 128))
    c = min(SCAN_CHUNK, tb)
    nblk = t // tb
    rb0 = row0 // tb
    last_pass = prev is not None
    slow, masks = _scan_constants(c, reverse)

    def tblk(i):
        return (nblk - 1 - i) if reverse else i

    hb = _pick_tile(heads, (SCAN_HEADS, 2, 1))

    def col(off, width, per_head=True):
        if not per_head:
            return pl.BlockSpec((tb, width), lambda b, h, i: (rb0 + b * nblk + tblk(i), off // width))
        assert off % (hb * width) == 0
        return pl.BlockSpec((tb, hb * width),
                            lambda b, h, i: (rb0 + b * nblk + tblk(i), off // (hb * width) + h))

    in_specs = [col(*spec) for spec in col_specs]
    args = [proj] * len(col_specs)
    for e in extra:
        in_specs.append(pl.BlockSpec((e.shape[0], hb * dk), lambda b, h, i: (0, h)))
        args.append(e)
    in_specs.append(pl.BlockSpec((None, hb, dk, dv), lambda b, h, i: (b, h, 0, 0)))
    args.append(s0)
    for cst in (slow, masks):
        in_specs.append(pl.BlockSpec(cst.shape, lambda b, h, i, nd=cst.ndim: (0,) * nd))
        args.append(cst)
    scratch = [pltpu.VMEM((hb, dv, dk), F32), pltpu.VMEM((hb * (tb // c), dv, dk), F32)]
    if last_pass:
        in_specs += [pl.BlockSpec((tb, hb * dv), lambda b, h, i: (b * nblk + tblk(i), h)),
                     col(gate_col, dv),
                     pl.BlockSpec((1, dv), lambda b, h, i: (0, 0))]
        args += [prev, proj, gn]
        scratch.append(pltpu.VMEM((tb, hb * dv), F32))
    y_dtype = BF16 if last_pass else F32
    return pl.pallas_call(
        functools.partial(kernel, c=c, reverse=reverse, last_pass=last_pass, **kw),
        grid=(nb, heads // hb, nblk),
        in_specs=in_specs,
        out_specs=[pl.BlockSpec((tb, hb * dv), lambda b, h, i: (b * nblk + tblk(i), h)),
                   pl.BlockSpec((None, hb, dk, dv), lambda b, h, i: (b, h, 0, 0))],
        out_shape=[jax.ShapeDtypeStruct((nb * t, heads * dv), y_dtype),
                   jax.ShapeDtypeStruct((nb, heads, dk, dv), F32)],
        scratch_shapes=scratch,
        compiler_params=_params(("parallel", "parallel", "arbitrary")),
    )(*args)


class _Geometry:
    pass


def _rope_tables(n_tok, da):
    ax = da // 2
    rows = n_tok // GRID_W
    r = jnp.broadcast_to(jnp.arange(rows, dtype=F32)[:, None], (rows, GRID_W)).reshape(-1)
    col = jnp.broadcast_to(jnp.arange(GRID_W, dtype=F32)[None, :], (rows, GRID_W)).reshape(-1)
    inv = ROPE_BASE ** (-jnp.arange(0, ax, 2, dtype=F32) / ax)
    ar, ac = r[:, None] * inv, col[:, None] * inv
    ang = jnp.concatenate([ar, ar, ac, ac], axis=-1)
    cos, sin = jnp.cos(ang), jnp.sin(ang)
    q = da // 4
    sign = jnp.where((jnp.arange(da) % (2 * q)) < q, -1.0, 1.0).astype(F32)
    return jnp.tile(cos, (1, 2)), jnp.tile(sin * sign, (1, 2))


def kernel(x_prompt, x_sample, c, cache_attn_k, cache_attn_v, state_hgrn, state_gla, c_ctx, ada_w, ada_b, norm_w, ffn_w_gate, ffn_w_up, ffn_w_down, w_in, attn_qk_norm, attn_lambda, attn_out_norm, hgrn_lb_logits, hgrn_out_norm, gla_w_decay, gla_b_decay, gla_out_norm, w_branch, w_out):
    batch, seq, d = x_prompt.shape
    dec_batch, dec_seq, _ = x_sample.shape
    depth = ada_w.shape[0]
    ha, da = cache_attn_k.shape[2], cache_attn_k.shape[5]
    dva = 2 * da
    hh, hdk, hdv = state_hgrn.shape[3:]
    hg, gdk, gdv = state_gla.shape[3:]
    rank = gla_w_decay.shape[2]
    sizes = (ha * 2 * da, ha * 2 * da, ha * dva, hh * hdk, hh * hdk, hh * hdk, hh * hdv, hh * hdv,
             hg * gdk, hg * gdk, hg * gdv, hg * gdv, 2 * rank, 3 * d)
    assert sum(sizes) == w_in.shape[2] and hdk == LANES and hdv == LANES and gdk == LANES and dva == LANES

    geo = _Geometry()
    geo.n_ctx_rows = batch * seq
    geo.dec_seq = dec_seq
    geo.tile_rows = math.gcd(geo.n_ctx_rows, dec_seq)
    n_ctx = geo.n_ctx_rows

    src = np.concatenate([[0], np.cumsum(sizes)])
    n_main = int(src[12])
    glr_pad = 2 * LANES
    names = ('aq', 'ak', 'av', 'hq', 'hf_f', 'hf_b', 'hi', 'hg', 'gq', 'gk', 'gv', 'gg')
    off = {nm: 3 * d + int(src[i]) for i, nm in enumerate(names)}
    off['glr'] = 3 * d + n_main

    def pack_w_in(w):
        return jnp.concatenate([w[:, src[13]:], w[:, :n_main], w[:, src[12]:src[13]],
                                jnp.zeros((d, glr_pad - 2 * rank), w.dtype)], axis=1).astype(BF16)

    n_cond = 1 + dec_batch
    cond = jnp.concatenate([c_ctx[None], c, jnp.zeros((-n_cond % 8, d), F32)], axis=0)
    mod = _adaln(cond, ada_w, ada_b).reshape(depth, cond.shape[0], 9, d)

    lb_w = jax.nn.softmax(hgrn_lb_logits.astype(F32), axis=0)
    lower_bounds = jnp.cumsum(lb_w, axis=0) - lb_w[0]
    rope_tabs = _rope_tables(dec_seq, da)

    x = jnp.concatenate([x_prompt.reshape(n_ctx, d), x_sample.reshape(dec_batch * dec_seq, d)], axis=0)
    zeros_h = jnp.zeros((batch, hh, hdk, hdv), F32)
    zeros_g = jnp.zeros((batch, hg, gdk, gdv), F32)
    ks_l, vs_l, sh_l, sg_l = [], [], [], []

    for l in range(depth):
        m_l = mod[l]
        nw = norm_w[l]
        x = _ffn(x, m_l, nw[0:1], ffn_w_gate[l, 0].astype(BF16), ffn_w_up[l, 0].astype(BF16),
                 ffn_w_down[l, 0].astype(BF16), 0, geo)
        proj = _inproj(x, m_l, nw[1:2], pack_w_in(w_in[l]), 3, geo)

        lam_init = 0.8 - 0.6 * math.exp(-0.3 * l)
        gains = jnp.tile(attn_qk_norm[l], (1, 2))
        cols = (off['aq'], off['ak'], off['av'])
        gn_a = attn_out_norm[l][:, None]
        qp, kp, vp, k32, v32 = _attn_prep(proj, cols, gains, None, batch, seq, ha, da, 0, True)
        ks_l.append(k32.reshape(batch, ha, seq, 2, da))
        vs_l.append(v32)
        qs, ksm, vsm = _attn_prep(proj, cols, gains, rope_tabs, dec_batch, dec_seq, ha, da, n_ctx, False)
        k_all = jnp.concatenate([cache_attn_k[:, l].reshape(dec_batch, ha, -1, 2 * da).astype(BF16), ksm], axis=2)
        v_all = jnp.concatenate([jnp.swapaxes(cache_attn_v[:, l], 2, 3).astype(BF16), vsm], axis=3)
        br_a = jnp.concatenate([_attention(qp, kp, vp, attn_lambda[l], gn_a, lam_init, da),
                                _attention(qs, k_all, v_all, attn_lambda[l], gn_a, lam_init, da)], axis=0)

        lb = lower_bounds[l][:, None]
        gn_h = hgrn_out_norm[l][None]

        def hgrn(nb, t, row0, s0):
            specs_b = ((off['hq'], hdk), (off['hf_b'], hdk), (off['hi'], hdv))
            specs_f = ((off['hq'], hdk), (off['hf_f'], hdk), (off['hi'], hdv))
            o_b, s_b = _scan_call(_hgrn_kernel, proj, specs_b, [lb[1]], s0[1], None, None, None,
                                  nb, t, hh, hdk, hdv, row0, True)
            y, s_f = _scan_call(_hgrn_kernel, proj, specs_f, [lb[0]], s0[0], o_b, off['hg'], gn_h,
                                nb, t, hh, hdk, hdv, row0, False)
            return y, jnp.stack([s_f, s_b], axis=1)

        y_p, s_h = hgrn(batch, seq, 0, (zeros_h, zeros_h))
        y_s, _ = hgrn(dec_batch, dec_seq, n_ctx, (state_hgrn[:, l, 0], state_hgrn[:, l, 1]))
        br_h = jnp.concatenate([y_p, y_s], axis=0)
        sh_l.append(s_h)

        wd = gla_w_decay[l]
        bd = gla_b_decay[l][:, None]
        gn_g = gla_out_norm[l][None]

        def gla(nb, t, row0, s0):
            specs = ((off['gq'], gdk), (off['gk'], gdk), (off['gv'], gdv), (off['glr'], LANES, False))
            kw = dict(rank=rank, q_scale=gdk ** -0.5)
            o_b, s_b = _scan_call(_gla_kernel, proj, specs, [wd[1], bd[1]], s0[1], None, None, None,
                                  nb, t, hg, gdk, gdv, row0, True, **kw)
            y, s_f = _scan_call(_gla_kernel, proj, specs, [wd[0], bd[0]], s0[0], o_b, off['gg'], gn_g,
                                nb, t, hg, gdk, gdv, row0, False, **kw)
            return y, jnp.stack([s_f, s_b], axis=1)

        y_p, s_g = gla(batch, seq, 0, (zeros_g, zeros_g))
        y_s, _ = gla(dec_batch, dec_seq, n_ctx, (state_gla[:, l, 0], state_gla[:, l, 1]))
        br_g = jnp.concatenate([y_p, y_s], axis=0)
        sg_l.append(s_g)

        merged = _merge(br_a, br_h, br_g, proj, w_branch[l].astype(BF16), geo)
        x = _outproj(merged, w_out[l].astype(BF16), x, m_l, 5, geo)
        x = _ffn(x, m_l, nw[2:3], ffn_w_gate[l, 1].astype(BF16), ffn_w_up[l, 1].astype(BF16),
                 ffn_w_down[l, 1].astype(BF16), 6, geo)

    y_prompt = x[:n_ctx].reshape(batch, seq, d)
    y_sample = x[n_ctx:].reshape(dec_batch, dec_seq, d)
    return (y_prompt, y_sample, jnp.stack(ks_l, axis=1), jnp.stack(vs_l, axis=1),
            jnp.stack(sh_l, axis=1), jnp.stack(sg_l, axis=1))
```

```python
import functools
import math

import jax
import jax.numpy as jnp
import numpy as np
from jax import lax
from jax.experimental import pallas as pl
from jax.experimental.pallas import tpu as pltpu

F32 = jnp.float32
BF16 = jnp.bfloat16

GRID_W = 64
GLA_TAU = 16.0
ROPE_BASE = 10000.0
EPS = 1e-6
LANES = 128
SCAN_CHUNK = 64
SCAN_BLOCK = 256
SCAN_HEADS = 4
ATTN_LOOKAHEAD = 2
SAFE_EXPONENT = 80.0
VMEM_LIMIT = 56 * 1024 * 1024


def _dot(a, b):
    return jnp.dot(a, b, preferred_element_type=F32)


def _dot_nt(a, b):
    return lax.dot_general(a, b, (((1,), (1,)), ((), ())), preferred_element_type=F32)


def _dot_tn(a, b):
    return lax.dot_general(a, b, (((0,), (0,)), ((), ())), preferred_element_type=F32)


def _sigmoid(x):
    return 1.0 / (1.0 + jnp.exp(-x))


def _silu(x):
    return x * _sigmoid(x)


def _split3(x):
    hi = x.astype(BF16)
    r1 = x - hi.astype(F32)
    mid = r1.astype(BF16)
    lo = (r1 - mid.astype(F32)).astype(BF16)
    return hi, mid, lo


def _dot_exact_lhs(m, parts):
    return _dot(m, parts[0]) + _dot(m, parts[1]) + _dot(m, parts[2])


def _pick_tile(n, candidates):
    for c in candidates:
        if n % c == 0:
            return c
    return n


def _params(sem):
    return pltpu.CompilerParams(dimension_semantics=sem, vmem_limit_bytes=VMEM_LIMIT)


def _adaln_kernel(c_ref, w_ref, b_ref, o_ref):
    s = _silu(c_ref[...]).astype(BF16)
    o_ref[...] = _dot(s, w_ref[...].astype(BF16)) + b_ref[...]


def _adaln(cond, ada_w, ada_b):
    depth, d, n = ada_w.shape
    rows = cond.shape[0]
    tn = _pick_tile(n, (1024, 512, 256, 128))
    return pl.pallas_call(
        _adaln_kernel,
        grid=(depth, n // tn),
        in_specs=[pl.BlockSpec((rows, d), lambda l, j: (0, 0)),
                  pl.BlockSpec((None, d, tn), lambda l, j: (l, 0, j)),
                  pl.BlockSpec((None, 1, tn), lambda l, j: (l, 0, j))],
        out_specs=pl.BlockSpec((None, rows, tn), lambda l, j: (l, 0, j)),
        out_shape=jax.ShapeDtypeStruct((depth, rows, n), F32),
        compiler_params=_params(("parallel", "parallel")),
    )(cond, ada_w, ada_b.reshape(depth, 1, n))


def _modulated(x, nw, shift, scale):
    y = x * lax.rsqrt(jnp.mean(x * x, axis=-1, keepdims=True) + EPS) * nw
    return y * (1.0 + scale) + shift


def _ffn_kernel(*refs, base, next_base, n_carry):
    x_ref, m_ref, nw_ref, wg_ref, wu_ref, wd_ref = refs[:6]
    pos = 6
    if next_base is not None:
        nw_next_ref = refs[pos]
        pos += 1
    pos += n_carry
    o_ref = refs[pos]
    if next_base is not None:
        h_next_ref = refs[pos + 1]
    h_ref = refs[-1]
    j = pl.program_id(1)

    @pl.when(j == 0)
    def _():
        h = _modulated(x_ref[...], nw_ref[...], m_ref[base:base + 1, :], m_ref[base + 1:base + 2, :])
        h_ref[...] = h.astype(BF16)
        o_ref[...] = jnp.zeros_like(o_ref)

    h = h_ref[...]
    g = _dot(h, wg_ref[...])
    u = _dot(h, wu_ref[...])
    o_ref[...] += _dot((_silu(g) * u).astype(BF16), wd_ref[...])

    @pl.when(j == pl.num_programs(1) - 1)
    def _():
        y = x_ref[...] + 0.5 * m_ref[base + 2:base + 3, :] * o_ref[...]
        o_ref[...] = y
        if next_base is not None:
            h_next_ref[...] = _modulated(y, nw_next_ref[...], m_ref[next_base:next_base + 1, :],
                                         m_ref[next_base + 1:next_base + 2, :]).astype(BF16)


def _cond_index(i, tm, row0, n_ctx_rows, seq_rows):
    r = row0 + i * tm
    return jnp.where(r < n_ctx_rows, 0, (r - n_ctx_rows) // seq_rows + 1)


def _ffn(x, mod, nw, wg, wu, wd, base, geo, *, rows=None, in_row0=0, token_row0=0, out_row0=0,
         out_rows=None, carry=(), nxt=None):
    d = x.shape[1]
    rows = x.shape[0] if rows is None else rows
    out_rows = rows if out_rows is None else out_rows
    f = wg.shape[1]
    tm = _pick_tile(geo.tile_rows, (512, 256, 128))
    tf = _pick_tile(f, (512, 256, 128))
    cidx = functools.partial(_cond_index, tm=tm, row0=token_row0, n_ctx_rows=geo.n_ctx_rows,
                             seq_rows=geo.dec_seq)
    ib, ob = in_row0 // tm, out_row0 // tm
    in_specs = [pl.BlockSpec((tm, d), lambda i, j: (i + ib, 0)),
                pl.BlockSpec((None, 9, d), lambda i, j: (cidx(i), 0, 0)),
                pl.BlockSpec((1, d), lambda i, j: (0, 0)),
                pl.BlockSpec((d, tf), lambda i, j: (0, j)),
                pl.BlockSpec((d, tf), lambda i, j: (0, j)),
                pl.BlockSpec((tf, d), lambda i, j: (j, 0))]
    args = [x, mod, nw, wg, wu, wd]
    out_specs = [pl.BlockSpec((tm, d), lambda i, j: (i + ob, 0))]
    out_shape = [jax.ShapeDtypeStruct((out_rows, d), F32)]
    if nxt is not None:
        in_specs.append(pl.BlockSpec((1, d), lambda i, j: (0, 0)))
        args.append(nxt[0])
        out_specs.append(pl.BlockSpec((tm, d), lambda i, j: (i + ob, 0)))
        out_shape.append(jax.ShapeDtypeStruct((out_rows, d), BF16))
    aliases = {len(args) + k: k for k in range(len(carry))}
    in_specs += [pl.BlockSpec(memory_space=pl.ANY)] * len(carry)
    args += list(carry)
    return pl.pallas_call(
        functools.partial(_ffn_kernel, base=base, next_base=None if nxt is None else nxt[1],
                          n_carry=len(carry)),
        grid=(rows // tm, f // tf),
        in_specs=in_specs,
        out_specs=out_specs,
        out_shape=out_shape,
        input_output_aliases=aliases,
        scratch_shapes=[pltpu.VMEM((tm, d), BF16)],
        compiler_params=_params(("parallel", "arbitrary")),
    )(*args)


def _matmul_kernel(a_ref, w_ref, o_ref, *, gate):
    y = _dot(a_ref[...], w_ref[...])
    o_ref[...] = (_sigmoid(y) if gate else y).astype(o_ref.dtype)


def _matmul(a, w, out_dtype, gate=False):
    nt, k = a.shape
    n = w.shape[1]
    tm = _pick_tile(nt, (1024, 512, 256, 128))
    tn = _pick_tile(n, (1024, 768, 512, 256, 128))
    return pl.pallas_call(
        functools.partial(_matmul_kernel, gate=gate),
        grid=(nt // tm, n // tn),
        in_specs=[pl.BlockSpec((tm, k), lambda i, j: (i, 0)),
                  pl.BlockSpec((k, tn), lambda i, j: (0, j))],
        out_specs=pl.BlockSpec((tm, tn), lambda i, j: (i, j)),
        out_shape=jax.ShapeDtypeStruct((nt, n), out_dtype),
        compiler_params=_params(("parallel", "parallel")),
    )(a, w)


def _mix_out_kernel(ba_ref, bh_ref, bg_ref, g_ref, wb_ref, wo_ref, x_ref, m_ref, o_ref, *, row):
    d = x_ref.shape[1]
    acc = g_ref[:, 0:d].astype(F32) * _dot(ba_ref[...], wb_ref[0])
    acc += g_ref[:, d:2 * d].astype(F32) * _dot(bh_ref[...], wb_ref[1])
    acc += g_ref[:, 2 * d:3 * d].astype(F32) * _dot(bg_ref[...], wb_ref[2])
    o_ref[...] = x_ref[...] + m_ref[row:row + 1, :] * _dot(acc.astype(BF16), wo_ref[...])


def _mix_out(br_a, br_h, br_g, gates, wb, wo, x, mod, row, geo):
    nt, bw = br_a.shape
    d = x.shape[1]
    tm = _pick_tile(geo.tile_rows, (256, 128))
    cidx = functools.partial(_cond_index, tm=tm, row0=0, n_ctx_rows=geo.n_ctx_rows, seq_rows=geo.dec_seq)
    bspec = pl.BlockSpec((tm, bw), lambda i: (i, 0))
    return pl.pallas_call(
        functools.partial(_mix_out_kernel, row=row),
        grid=(nt // tm,),
        in_specs=[bspec, bspec, bspec,
                  pl.BlockSpec((tm, 3 * d), lambda i: (i, 0)),
                  pl.BlockSpec((3, bw, d), lambda i: (0, 0, 0), pipeline_mode=pl.Buffered(1)),
                  pl.BlockSpec((d, d), lambda i: (0, 0), pipeline_mode=pl.Buffered(1)),
                  pl.BlockSpec((tm, d), lambda i: (i, 0)),
                  pl.BlockSpec((None, 9, d), lambda i: (cidx(i), 0, 0))],
        out_specs=pl.BlockSpec((tm, d), lambda i: (i, 0)),
        out_shape=jax.ShapeDtypeStruct((nt, d), F32),
        compiler_params=_params(("parallel",)),
    )(br_a, br_h, br_g, gates, wb, wo, x, mod)


def _half_mean_matrix(da):
    m = np.kron(np.eye(2), np.ones((da, da))) / da
    return jnp.asarray(m, BF16)


def _qk_normed(x, gmat, gain):
    sq = x * x
    hi = sq.astype(BF16)
    lo = (sq - hi.astype(F32)).astype(BF16)
    msq = _dot(hi, gmat) + _dot(lo, gmat)
    return x * lax.rsqrt(msq + EPS) * gain


def _rotated(x, cos, sin_signed, da):
    q = da // 4
    width = x.shape[-1]
    lane = lax.broadcasted_iota(jnp.int32, x.shape, 1)
    first = (lane % (2 * q)) < q
    partner = jnp.where(first, pltpu.roll(x, width - q, axis=1), pltpu.roll(x, q, axis=1))
    return x * cos + partner * sin_signed


def _attn_prep_kernel(*refs, rope, emit_f32, da):
    if rope:
        q_ref, k_ref, v_ref, gm_ref, gn_ref, cos_ref, sin_ref = refs[:7]
        outs = refs[7:]
    else:
        q_ref, k_ref, v_ref, gm_ref, gn_ref = refs[:5]
        outs = refs[5:]
    gm = gm_ref[...]
    q = _qk_normed(q_ref[...], gm, gn_ref[0:1, :])
    k = _qk_normed(k_ref[...], gm, gn_ref[1:2, :])
    if rope:
        q = _rotated(q, cos_ref[...], sin_ref[...], da)
        k = _rotated(k, cos_ref[...], sin_ref[...], da)
    v = v_ref[...]
    outs[0][...] = (q * (math.log2(math.e) * da ** -0.5)).T.astype(BF16)
    outs[1][...] = k.astype(BF16)
    outs[2][...] = v.T.astype(BF16)
    if emit_f32:
        outs[3][...] = k
        outs[4][...] = v


def _attn_prep(proj, cols, gains, rope_tabs, nb, t, heads, da, row0, emit_f32):
    w = 2 * da
    tt = _pick_tile(t, (512, 256, 128))
    nblk = t // tt
    rb0 = row0 // tt
    cq, ck, cv = (c // w for c in cols)
    rope = rope_tabs is not None

    def col(c0):
        return pl.BlockSpec((tt, w), lambda b, h, i: (rb0 + b * nblk + i, c0 + h))

    in_specs = [col(cq), col(ck), col(cv),
                pl.BlockSpec((w, w), lambda b, h, i: (0, 0)),
                pl.BlockSpec((2, w), lambda b, h, i: (0, 0))]
    args = [proj, proj, proj, _half_mean_matrix(da), gains]
    if rope:
        in_specs += [pl.BlockSpec((tt, w), lambda b, h, i: (i, 0))] * 2
        args += list(rope_tabs)
    rows = pl.BlockSpec((None, None, tt, w), lambda b, h, i: (b, h, i, 0))
    cols_t = pl.BlockSpec((None, None, w, tt), lambda b, h, i: (b, h, 0, i))
    out_specs = [cols_t, rows, cols_t]
    out_shape = [jax.ShapeDtypeStruct((nb, heads, w, t), BF16),
                 jax.ShapeDtypeStruct((nb, heads, t, w), BF16),
                 jax.ShapeDtypeStruct((nb, heads, w, t), BF16)]
    if emit_f32:
        out_specs += [rows, rows]
        out_shape += [jax.ShapeDtypeStruct((nb, heads, t, w), F32)] * 2
    return pl.pallas_call(
        functools.partial(_attn_prep_kernel, rope=rope, emit_f32=emit_f32, da=da),
        grid=(nb, heads, nblk),
        in_specs=in_specs,
        out_specs=out_specs,
        out_shape=out_shape,
        compiler_params=_params(("parallel", "parallel", "parallel")),
    )(*args)


def _attn_kernel(qt_ref, k_ref, vt_ref, lam_ref, g_ref, o_ref, *, tk, lam_init, da):
    nk = k_ref.shape[0] // tk
    qt = qt_ref[...]
    feat = lax.broadcasted_iota(jnp.int32, qt.shape, 0)
    zero = jnp.zeros_like(qt)

    tq = qt.shape[1]
    q2t = jnp.concatenate([jnp.where(feat < da, qt, zero), jnp.where(feat >= da, qt, zero)], axis=1)

    def scores(i):
        return _dot(k_ref[i * tk:(i + 1) * tk, :], q2t)

    ahead = [scores(i) for i in range(min(ATTN_LOOKAHEAD, nk))]
    m = l = acc = None
    for i in range(nk):
        s = ahead.pop(0)
        if i + ATTN_LOOKAHEAD < nk:
            ahead.append(scores(i + ATTN_LOOKAHEAD))
        m_tile = jnp.max(s, axis=0, keepdims=True)
        m_new = m_tile if i == 0 else jnp.maximum(m, m_tile)
        p = jnp.exp2(s - m_new)
        pv = _dot(vt_ref[:, i * tk:(i + 1) * tk], p.astype(BF16))
        if i == 0:
            l, acc = jnp.sum(p, axis=0, keepdims=True), pv
        else:
            alpha = jnp.exp2(m - m_new)
            l = alpha * l + jnp.sum(p, axis=0, keepdims=True)
            acc = alpha * acc + pv
        m = m_new
    o = acc / l
    lam_p = lam_ref[...]
    lam = (jnp.exp(jnp.sum(lam_p[0:1, :] * lam_p[1:2, :], axis=-1, keepdims=True))
           - jnp.exp(jnp.sum(lam_p[2:3, :] * lam_p[3:4, :], axis=-1, keepdims=True)) + lam_init)
    o = o[:, :tq] - lam * o[:, tq:]
    o = o * lax.rsqrt(jnp.mean(o * o, axis=0, keepdims=True) + EPS) * g_ref[...]
    o_ref[...] = (o * (1.0 - lam_init)).T.astype(BF16)


def _attention(qt, k, vt, lam_p, gain, lam_init, da):
    nb, heads, w, n = qt.shape
    m = k.shape[2]
    tq = _pick_tile(n, (256, 128))
    tk = _pick_tile(m, (256, 128))
    nq = n // tq
    return pl.pallas_call(
        functools.partial(_attn_kernel, tk=tk, lam_init=lam_init, da=da),
        grid=(nb, heads, nq),
        in_specs=[pl.BlockSpec((None, None, w, tq), lambda b, h, i: (b, h, 0, i)),
                  pl.BlockSpec((None, None, m, w), lambda b, h, i: (b, h, 0, 0)),
                  pl.BlockSpec((None, None, w, m), lambda b, h, i: (b, h, 0, 0)),
                  pl.BlockSpec(lam_p.shape, lambda b, h, i: (0, 0)),
                  pl.BlockSpec((w, 1), lambda b, h, i: (0, 0))],
        out_specs=pl.BlockSpec((tq, w), lambda b, h, i: (b * nq + i, h)),
        out_shape=jax.ShapeDtypeStruct((nb * n, heads * w), BF16),
        compiler_params=_params(("parallel", "parallel", "parallel")),
    )(qt, k, vt, lam_p, gain)


def _scan_constants(c, reverse):
    i = np.arange(c)[:, None]
    t = np.arange(c)[None, :]

    levels, masks = [], []
    g = c
    while g >= 2:
        h = g // 2
        mat = np.zeros((c, c), np.float32)
        for r in range(c):
            mid = (r // g) * g + h
            if not reverse:
                if r >= mid:
                    mat[r, mid:r + 1] = 1.0
                else:
                    mat[r, r + 1:mid] = 1.0
            else:
                if r < mid:
                    mat[r, r:mid] = 1.0
                else:
                    mat[r, mid:r] = 1.0
        same = (i // g) == (t // g)
        if not reverse:
            ok = same & ((i % g) >= h) & ((t % g) < h)
        else:
            ok = same & ((i % g) < h) & ((t % g) >= h)
        levels.append(mat)
        masks.append(ok.astype(np.float32))
        g = h
    masks.append(np.eye(c, dtype=np.float32))
    return jnp.asarray(np.concatenate(levels, axis=0), BF16), jnp.asarray(np.stack(masks), F32)


def _chunk_cumsum(x, c, reverse):
    tb = x.shape[0]
    pos = lax.broadcasted_iota(jnp.int32, x.shape, 0) % c
    s = 1
    while s < c:
        if reverse:
            x = x + jnp.where(pos < c - s, pltpu.roll(x, tb - s, axis=0), 0.0)
        else:
            x = x + jnp.where(pos >= s, pltpu.roll(x, s, axis=0), 0.0)
        s *= 2
    return x


def _scan_block(q, k, v, la, slow_ref, mask_ref, st_ref, u_ref, o_ref, *, c, reverse):
    tb = q.shape[0]
    n = tb // c
    hb, dv, dk = st_ref.shape
    chunks = [slice(ci * c, (ci + 1) * c) for ci in range(n)]
    kcols = [slice(j * dk, (j + 1) * dk) for j in range(hb)]
    vcols = [slice(j * dv, (j + 1) * dv) for j in range(hb)]
    ref_row = c // 2 if reverse else c // 2 - 1
    end_row = 0 if reverse else c - 1
    b = _chunk_cumsum(la, c, reverse)
    b_ref = jnp.concatenate([jnp.broadcast_to(b[sl][ref_row:ref_row + 1], (c, b.shape[1])) for sl in chunks], axis=0)
    b_end = jnp.concatenate([jnp.broadcast_to(b[sl][end_row:end_row + 1], (c, b.shape[1])) for sl in chunks], axis=0)
    zq = b - b_ref
    safe = jnp.max(jnp.abs(zq)) < SAFE_EXPONENT
    q_in = (q * jnp.exp(b)).astype(BF16)
    k_up = (k * jnp.exp(b_end - b)).astype(BF16)
    v_bf = v.astype(BF16)
    row = lax.broadcasted_iota(jnp.int32, (c, c), 0)
    col = lax.broadcasted_iota(jnp.int32, (c, c), 1)
    visible = (col >= row) if reverse else (col <= row)
    n_levels = slow_ref.shape[0] // c

    def fast_scores():
        q_mid = (q * jnp.exp(zq)).astype(BF16)
        k_mid = (k * jnp.exp(-zq)).astype(BF16)
        return jnp.concatenate([jnp.where(visible, _dot_nt(q_mid[sl, kc], k_mid[sl, kc]), 0.0)
                                for kc in kcols for sl in chunks], axis=0)

    def slow_scores():
        q_bf = q.astype(BF16)
        k_bf = k.astype(BF16)
        parts = _split3(la)
        out = []
        for kc in kcols:
            for sl in chunks:
                zs = _dot_exact_lhs(slow_ref[...], tuple(p[sl, kc] for p in parts))
                a = mask_ref[n_levels] * _dot_nt(q_bf[sl, kc], k_bf[sl, kc])
                for lv in range(n_levels):
                    e = jnp.exp(zs[lv * c:(lv + 1) * c])
                    a += mask_ref[lv] * _dot_nt((q[sl, kc] * e).astype(BF16), (k[sl, kc] * e).astype(BF16))
                out.append(a)
        return jnp.concatenate(out, axis=0)

    a = lax.cond(safe, fast_scores, slow_scores).astype(BF16)
    for j in range(hb):
        for ci, sl in enumerate(chunks):
            u_ref[j * n + ci] = _dot_tn(v_bf[sl, vcols[j]], k_up[sl, kcols[j]])
    entry = {}
    for j in range(hb):
        st = st_ref[j]
        for ci in (range(n - 1, -1, -1) if reverse else range(n)):
            entry[j, ci] = st.astype(BF16)
            st = st * jnp.exp(b_end[chunks[ci], kcols[j]][0:1]) + u_ref[j * n + ci]
        st_ref[j] = st
    for j in range(hb):
        for ci, sl in enumerate(chunks):
            rows = slice(j * tb + ci * c, j * tb + (ci + 1) * c)
            o_ref[sl, vcols[j]] = _dot_nt(q_in[sl, kcols[j]], entry[j, ci]) + _dot(a[rows], v_bf[sl, vcols[j]])


def _scan_finish(o_ref, prev_ref, gate_ref, gn_ref, y_ref):
    dv = gn_ref.shape[1]
    o = o_ref[...] + prev_ref[...]
    gate = _silu(gate_ref[...])
    for j in range(o.shape[1] // dv):
        cols = slice(j * dv, (j + 1) * dv)
        oj = o[:, cols]
        oj = oj * lax.rsqrt(jnp.mean(oj * oj, axis=-1, keepdims=True) + EPS) * gn_ref[...]
        y_ref[:, cols] = (oj * gate[:, cols]).astype(BF16)


def _scan_state_io(s0_ref, sT_ref, st_ref, do_body):
    t = pl.program_id(2)
    hb = st_ref.shape[0]

    @pl.when(t == 0)
    def _():
        for j in range(hb):
            st_ref[j] = s0_ref[j].T

    do_body()

    @pl.when(t == pl.num_programs(2) - 1)
    def _():
        for j in range(hb):
            sT_ref[j] = st_ref[j].T


def _hgrn_kernel(*refs, c, reverse, last_pass):
    if last_pass:
        (q_ref, f_ref, i_ref, lb_ref, s0_ref, slow_ref, mask_ref, prev_ref, gate_ref, gn_ref,
         y_ref, sT_ref, st_ref, u_ref, o_scr) = refs
    else:
        (q_ref, f_ref, i_ref, lb_ref, s0_ref, slow_ref, mask_ref,
         y_ref, sT_ref, st_ref, u_ref) = refs
        o_scr = y_ref

    def body():
        lb = lb_ref[...]
        f = lb + (1.0 - lb) * _sigmoid(f_ref[...])
        _scan_block(_silu(q_ref[...]), 1.0 - f, i_ref[...], jnp.log(f),
                    slow_ref, mask_ref, st_ref, u_ref, o_scr, c=c, reverse=reverse)
        if last_pass:
            _scan_finish(o_scr, prev_ref, gate_ref, gn_ref, y_ref)

    _scan_state_io(s0_ref, sT_ref, st_ref, body)


def _gla_kernel(*refs, c, reverse, last_pass, rank, q_scale):
    if last_pass:
        (q_ref, k_ref, v_ref, r_ref, wd_ref, bd_ref, s0_ref, slow_ref, mask_ref,
         prev_ref, gate_ref, gn_ref, y_ref, sT_ref, st_ref, u_ref, o_scr) = refs
    else:
        (q_ref, k_ref, v_ref, r_ref, wd_ref, bd_ref, s0_ref, slow_ref, mask_ref,
         y_ref, sT_ref, st_ref, u_ref) = refs
        o_scr = y_ref

    def body():
        lo = rank if reverse else 0
        r = r_ref[...][:, lo:lo + rank].astype(BF16)
        logit = _dot(r, wd_ref[...].astype(BF16)) + bd_ref[...]
        la = (jnp.minimum(logit, 0.0) - jnp.log(1.0 + jnp.exp(-jnp.abs(logit)))) / GLA_TAU
        _scan_block(q_ref[...] * q_scale, k_ref[...], v_ref[...], la,
                    slow_ref, mask_ref, st_ref, u_ref, o_scr, c=c, reverse=reverse)
        if last_pass:
            _scan_finish(o_scr, prev_ref, gate_ref, gn_ref, y_ref)

    _scan_state_io(s0_ref, sT_ref, st_ref, body)


def _scan_call(kernel, proj, col_specs, extra, s0, prev, gate_col, gn, nb, t, heads, dk, dv, row0,
               reverse, **kw):
    tb = _pick_tile(t, (SCAN_BLOCK, 128))
    c = min(SCAN_CHUNK, tb)
    nblk = t // tb
    rb0 = row0 // tb
    last_pass = prev is not None
    slow, masks = _scan_constants(c, reverse)

    def tblk(i):
        return (nblk - 1 - i) if reverse else i

    hb = _pick_tile(heads, (SCAN_HEADS, 2, 1))

    def col(off, width, per_head=True):
        if not per_head:
            return pl.BlockSpec((tb, width), lambda b, h, i: (rb0 + b * nblk + tblk(i), off // width))
        assert off % (hb * width) == 0
        return pl.BlockSpec((tb, hb * width),
                            lambda b, h, i: (rb0 + b * nblk + tblk(i), off // (hb * width) + h))

    in_specs = [col(*spec[1:]) for spec in col_specs]
    args = [spec[0] for spec in col_specs]
    for e in extra:
        in_specs.append(pl.BlockSpec((e.shape[0], hb * dk), lambda b, h, i: (0, h)))
        args.append(e)
    in_specs.append(pl.BlockSpec((None, hb, dk, dv), lambda b, h, i: (b, h, 0, 0)))
    args.append(s0)
    for cst in (slow, masks):
        in_specs.append(pl.BlockSpec(cst.shape, lambda b, h, i, nd=cst.ndim: (0,) * nd))
        args.append(cst)
    scratch = [pltpu.VMEM((hb, dv, dk), F32), pltpu.VMEM((hb * (tb // c), dv, dk), F32)]
    if last_pass:
        in_specs += [pl.BlockSpec((tb, hb * dv), lambda b, h, i: (b * nblk + tblk(i), h)),
                     col(gate_col, dv),
                     pl.BlockSpec((1, dv), lambda b, h, i: (0, 0))]
        args += [prev, proj, gn]
        scratch.append(pltpu.VMEM((tb, hb * dv), F32))
    y_dtype = BF16 if last_pass else F32
    return pl.pallas_call(
        functools.partial(kernel, c=c, reverse=reverse, last_pass=last_pass, **kw),
        grid=(nb, heads // hb, nblk),
        in_specs=in_specs,
        out_specs=[pl.BlockSpec((tb, hb * dv), lambda b, h, i: (b * nblk + tblk(i), h)),
                   pl.BlockSpec((None, hb, dk, dv), lambda b, h, i: (b, h, 0, 0))],
        out_shape=[jax.ShapeDtypeStruct((nb * t, heads * dv), y_dtype),
                   jax.ShapeDtypeStruct((nb, heads, dk, dv), F32)],
        scratch_shapes=scratch,
        compiler_params=_params(("parallel", "parallel", "arbitrary")),
    )(*args)


class _Geometry:
    pass


def _rope_tables(n_tok, da):
    ax = da // 2
    rows = n_tok // GRID_W
    r = jnp.broadcast_to(jnp.arange(rows, dtype=F32)[:, None], (rows, GRID_W)).reshape(-1)
    col = jnp.broadcast_to(jnp.arange(GRID_W, dtype=F32)[None, :], (rows, GRID_W)).reshape(-1)
    inv = ROPE_BASE ** (-jnp.arange(0, ax, 2, dtype=F32) / ax)
    ar, ac = r[:, None] * inv, col[:, None] * inv
    ang = jnp.concatenate([ar, ar, ac, ac], axis=-1)
    cos, sin = jnp.cos(ang), jnp.sin(ang)
    q = da // 4
    sign = jnp.where((jnp.arange(da) % (2 * q)) < q, -1.0, 1.0).astype(F32)
    return jnp.tile(cos, (1, 2)), jnp.tile(sin * sign, (1, 2))


def kernel(x_prompt, x_sample, c, cache_attn_k, cache_attn_v, state_hgrn, state_gla, c_ctx, ada_w, ada_b, norm_w, ffn_w_gate, ffn_w_up, ffn_w_down, w_in, attn_qk_norm, attn_lambda, attn_out_norm, hgrn_lb_logits, hgrn_out_norm, gla_w_decay, gla_b_decay, gla_out_norm, w_branch, w_out):
    batch, seq, d = x_prompt.shape
    dec_batch, dec_seq, _ = x_sample.shape
    depth = ada_w.shape[0]
    ha, da = cache_attn_k.shape[2], cache_attn_k.shape[5]
    dva = 2 * da
    hh, hdk, hdv = state_hgrn.shape[3:]
    hg, gdk, gdv = state_gla.shape[3:]
    rank = gla_w_decay.shape[2]
    sizes = (ha * 2 * da, ha * 2 * da, ha * dva, hh * hdk, hh * hdk, hh * hdk, hh * hdv, hh * hdv,
             hg * gdk, hg * gdk, hg * gdv, hg * gdv, 2 * rank, 3 * d)
    assert sum(sizes) == w_in.shape[2] and hdk == LANES and hdv == LANES and gdk == LANES and dva == LANES

    geo = _Geometry()
    geo.n_ctx_rows = batch * seq
    geo.dec_seq = dec_seq
    geo.tile_rows = math.gcd(geo.n_ctx_rows, dec_seq)
    n_ctx = geo.n_ctx_rows

    src = [0] + [int(v) for v in np.cumsum(sizes)]
    n_main = src[12]
    names = ('aq', 'ak', 'av', 'hq', 'hf_f', 'hf_b', 'hi', 'hg', 'gq', 'gk', 'gv', 'gg')
    off = {nm: src[i] for i, nm in enumerate(names)}

    n_cond = 1 + dec_batch
    cond = jnp.concatenate([c_ctx[None], c, jnp.zeros((-n_cond % 8, d), F32)], axis=0)
    mod = _adaln(cond, ada_w, ada_b).reshape(depth, cond.shape[0], 9, d)

    lb_w = jax.nn.softmax(hgrn_lb_logits.astype(F32), axis=0)
    lower_bounds = jnp.cumsum(lb_w, axis=0) - lb_w[0]
    rope_tabs = _rope_tables(dec_seq, da)

    n_lat = dec_batch * dec_seq
    nt = n_ctx + n_lat
    zeros_h = jnp.zeros((batch, hh, hdk, hdv), F32)
    zeros_g = jnp.zeros((batch, hg, gdk, gdv), F32)
    ks_l, vs_l, sh_l, sg_l = [], [], [], []
    x = None

    for l in range(depth):
        m_l = mod[l]
        nw = norm_w[l]
        ffn1 = (m_l, nw[0:1], ffn_w_gate[l, 0].astype(BF16), ffn_w_up[l, 0].astype(BF16),
                ffn_w_down[l, 0].astype(BF16), 0, geo)
        if l == 0:
            part = _ffn(x_prompt.reshape(n_ctx, d), *ffn1, out_rows=nt, nxt=(nw[1:2], 3))
            x, h_mix = _ffn(x_sample.reshape(n_lat, d), *ffn1, token_row0=n_ctx, out_row0=n_ctx,
                            out_rows=nt, carry=part, nxt=(nw[1:2], 3))
        else:
            x, h_mix = _ffn(x, *ffn1, nxt=(nw[1:2], 3))
        w_l = w_in[l]
        proj = _matmul(h_mix, w_l[:, :n_main].astype(BF16), F32)
        w_glr = jnp.pad(w_l[:, src[12]:src[13]], ((0, 0), (0, LANES - 2 * rank))).astype(BF16)
        glr = _matmul(h_mix, w_glr, F32)
        gates = _matmul(h_mix, w_l[:, src[13]:].astype(BF16), BF16, gate=True)

        lam_init = 0.8 - 0.6 * math.exp(-0.3 * l)
        gains = jnp.tile(attn_qk_norm[l], (1, 2))
        cols = (off['aq'], off['ak'], off['av'])
        gn_a = attn_out_norm[l][:, None]
        qp, kp, vp, k32, v32 = _attn_prep(proj, cols, gains, None, batch, seq, ha, da, 0, True)
        ks_l.append(k32.reshape(batch, ha, seq, 2, da))
        vs_l.append(v32)
        qs, ksm, vsm = _attn_prep(proj, cols, gains, rope_tabs, dec_batch, dec_seq, ha, da, n_ctx, False)
        k_all = jnp.concatenate([cache_attn_k[:, l].reshape(dec_batch, ha, -1, 2 * da).astype(BF16), ksm], axis=2)
        v_all = jnp.concatenate([jnp.swapaxes(cache_attn_v[:, l], 2, 3).astype(BF16), vsm], axis=3)
        br_a = jnp.concatenate([_attention(qp, kp, vp, attn_lambda[l], gn_a, lam_init, da),
                                _attention(qs, k_all, v_all, attn_lambda[l], gn_a, lam_init, da)], axis=0)

        lb = lower_bounds[l][:, None]
        gn_h = hgrn_out_norm[l][None]

        def hgrn(nb, t, row0, s0):
            specs_b = ((proj, off['hq'], hdk), (proj, off['hf_b'], hdk), (proj, off['hi'], hdv))
            specs_f = ((proj, off['hq'], hdk), (proj, off['hf_f'], hdk), (proj, off['hi'], hdv))
            o_b, s_b = _scan_call(_hgrn_kernel, proj, specs_b, [lb[1]], s0[1], None, None, None,
                                  nb, t, hh, hdk, hdv, row0, True)
            y, s_f = _scan_call(_hgrn_kernel, proj, specs_f, [lb[0]], s0[0], o_b, off['hg'], gn_h,
                                nb, t, hh, hdk, hdv, row0, False)
            return y, jnp.stack([s_f, s_b], axis=1)

        y_p, s_h = hgrn(batch, seq, 0, (zeros_h, zeros_h))
        y_s, _ = hgrn(dec_batch, dec_seq, n_ctx, (state_hgrn[:, l, 0], state_hgrn[:, l, 1]))
        br_h = jnp.concatenate([y_p, y_s], axis=0)
        sh_l.append(s_h)

        wd = gla_w_decay[l]
        bd = gla_b_decay[l][:, None]
        gn_g = gla_out_norm[l][None]

        def gla(nb, t, row0, s0):
            specs = ((proj, off['gq'], gdk), (proj, off['gk'], gdk), (proj, off['gv'], gdv),
                     (glr, 0, LANES, False))
            kw = dict(rank=rank, q_scale=gdk ** -0.5)
            o_b, s_b = _scan_call(_gla_kernel, proj, specs, [wd[1], bd[1]], s0[1], None, None, None,
                                  nb, t, hg, gdk, gdv, row0, True, **kw)
            y, s_f = _scan_call(_gla_kernel, proj, specs, [wd[0], bd[0]], s0[0], o_b, off['gg'], gn_g,
                                nb, t, hg, gdk, gdv, row0, False, **kw)
            return y, jnp.stack([s_f, s_b], axis=1)

        y_p, s_g = gla(batch, seq, 0, (zeros_g, zeros_g))
        y_s, _ = gla(dec_batch, dec_seq, n_ctx, (state_gla[:, l, 0], state_gla[:, l, 1]))
        br_g = jnp.concatenate([y_p, y_s], axis=0)
        sg_l.append(s_g)

        x = _mix_out(br_a, br_h, br_g, gates, w_branch[l].astype(BF16), w_out[l].astype(BF16), x, m_l, 5, geo)
        ffn2 = (m_l, nw[2:3], ffn_w_gate[l, 1].astype(BF16), ffn_w_up[l, 1].astype(BF16),
                ffn_w_down[l, 1].astype(BF16), 6, geo)
        if l + 1 < depth:
            x = _ffn(x, *ffn2)[0]
        else:
            y_prompt = _ffn(x, *ffn2, rows=n_ctx)[0].reshape(batch, seq, d)
            y_sample = _ffn(x, *ffn2, rows=n_lat, in_row0=n_ctx,
                            token_row0=n_ctx)[0].reshape(dec_batch, dec_seq, d)

    return (y_prompt, y_sample, jnp.stack(ks_l, axis=1), jnp.stack(vs_l, axis=1),
            jnp.stack(sh_l, axis=1), jnp.stack(sg_l, axis=1))
```

```python
import functools
import math

import jax
import jax.numpy as jnp
import numpy as np
from jax import lax
from jax.experimental import pallas as pl
from jax.experimental.pallas import tpu as pltpu

F32 = jnp.float32
BF16 = jnp.bfloat16

GRID_W = 64
GLA_TAU = 16.0
ROPE_BASE = 10000.0
EPS = 1e-6
LANES = 128
SCAN_CHUNK = 64
SCAN_BLOCK = 256
SCAN_HEADS = 4
ATTN_LOOKAHEAD = 2
SAFE_EXPONENT = 80.0
VMEM_LIMIT = 56 * 1024 * 1024


def _dot(a, b):
    return jnp.dot(a, b, preferred_element_type=F32)


def _dot_nt(a, b):
    return lax.dot_general(a, b, (((1,), (1,)), ((), ())), preferred_element_type=F32)


def _dot_tn(a, b):
    return lax.dot_general(a, b, (((0,), (0,)), ((), ())), preferred_element_type=F32)


def _sigmoid(x):
    return 1.0 / (1.0 + jnp.exp(-x))


def _silu(x):
    return x * _sigmoid(x)


def _split3(x):
    hi = x.astype(BF16)
    r1 = x - hi.astype(F32)
    mid = r1.astype(BF16)
    lo = (r1 - mid.astype(F32)).astype(BF16)
    return hi, mid, lo


def _dot_exact_lhs(m, parts):
    return _dot(m, parts[0]) + _dot(m, parts[1]) + _dot(m, parts[2])


def _pick_tile(n, candidates):
    for c in candidates:
        if n % c == 0:
            return c
    return n


def _params(sem):
    return pltpu.CompilerParams(dimension_semantics=sem, vmem_limit_bytes=VMEM_LIMIT)


def _adaln_kernel(c_ref, w_ref, b_ref, o_ref):
    s = _silu(c_ref[...]).astype(BF16)
    o_ref[...] = _dot(s, w_ref[...].astype(BF16)) + b_ref[...]


def _adaln(cond, ada_w, ada_b):
    depth, d, n = ada_w.shape
    rows = cond.shape[0]
    tn = _pick_tile(n, (1024, 512, 256, 128))
    return pl.pallas_call(
        _adaln_kernel,
        grid=(depth, n // tn),
        in_specs=[pl.BlockSpec((rows, d), lambda l, j: (0, 0)),
                  pl.BlockSpec((None, d, tn), lambda l, j: (l, 0, j)),
                  pl.BlockSpec((None, 1, tn), lambda l, j: (l, 0, j))],
        out_specs=pl.BlockSpec((None, rows, tn), lambda l, j: (l, 0, j)),
        out_shape=jax.ShapeDtypeStruct((depth, rows, n), F32),
        compiler_params=_params(("parallel", "parallel")),
    )(cond, ada_w, ada_b.reshape(depth, 1, n))


def _modulated(x, nw, shift, scale):
    y = x * lax.rsqrt(jnp.mean(x * x, axis=-1, keepdims=True) + EPS) * nw
    return y * (1.0 + scale) + shift


def _ffn_kernel(*refs, base, next_base, n_carry):
    x_ref, m_ref, nw_ref, wg_ref, wu_ref, wd_ref = refs[:6]
    pos = 6
    if next_base is not None:
        nw_next_ref = refs[pos]
        pos += 1
    pos += n_carry
    o_ref = refs[pos]
    if next_base is not None:
        h_next_ref = refs[pos + 1]
    h_ref = refs[-1]
    j = pl.program_id(1)

    @pl.when(j == 0)
    def _():
        h = _modulated(x_ref[...], nw_ref[...], m_ref[base:base + 1, :], m_ref[base + 1:base + 2, :])
        h_ref[...] = h.astype(BF16)
        o_ref[...] = jnp.zeros_like(o_ref)

    h = h_ref[...]
    g = _dot(h, wg_ref[...])
    u = _dot(h, wu_ref[...])
    o_ref[...] += _dot((_silu(g) * u).astype(BF16), wd_ref[...])

    @pl.when(j == pl.num_programs(1) - 1)
    def _():
        y = x_ref[...] + 0.5 * m_ref[base + 2:base + 3, :] * o_ref[...]
        o_ref[...] = y
        if next_base is not None:
            h_next_ref[...] = _modulated(y, nw_next_ref[...], m_ref[next_base:next_base + 1, :],
                                         m_ref[next_base + 1:next_base + 2, :]).astype(BF16)


def _cond_index(i, tm, row0, n_ctx_rows, seq_rows):
    r = row0 + i * tm
    return jnp.where(r < n_ctx_rows, 0, (r - n_ctx_rows) // seq_rows + 1)


def _ffn(x, mod, nw, wg, wu, wd, wsel, base, geo, *, rows=None, in_row0=0, token_row0=0, out_row0=0,
         out_rows=None, carry=(), nxt=None):
    d = x.shape[1]
    rows = x.shape[0] if rows is None else rows
    out_rows = rows if out_rows is None else out_rows
    f = wg.shape[-1]
    tm = _pick_tile(geo.tile_rows, (512, 256, 128))
    tf = _pick_tile(f, (512, 256, 128))
    cidx = functools.partial(_cond_index, tm=tm, row0=token_row0, n_ctx_rows=geo.n_ctx_rows,
                             seq_rows=geo.dec_seq)
    ib, ob = in_row0 // tm, out_row0 // tm
    l, s = wsel
    in_specs = [pl.BlockSpec((tm, d), lambda i, j: (i + ib, 0)),
                pl.BlockSpec((None, 9, d), lambda i, j: (cidx(i), 0, 0)),
                pl.BlockSpec((1, d), lambda i, j: (0, 0)),
                pl.BlockSpec((None, None, d, tf), lambda i, j: (l, s, 0, j)),
                pl.BlockSpec((None, None, d, tf), lambda i, j: (l, s, 0, j)),
                pl.BlockSpec((None, None, tf, d), lambda i, j: (l, s, j, 0))]
    args = [x, mod, nw, wg, wu, wd]
    out_specs = [pl.BlockSpec((tm, d), lambda i, j: (i + ob, 0))]
    out_shape = [jax.ShapeDtypeStruct((out_rows, d), F32)]
    if nxt is not None:
        in_specs.append(pl.BlockSpec((1, d), lambda i, j: (0, 0)))
        args.append(nxt[0])
        out_specs.append(pl.BlockSpec((tm, d), lambda i, j: (i + ob, 0)))
        out_shape.append(jax.ShapeDtypeStruct((out_rows, d), BF16))
    aliases = {len(args) + k: k for k in range(len(carry))}
    in_specs += [pl.BlockSpec(memory_space=pl.ANY)] * len(carry)
    args += list(carry)
    return pl.pallas_call(
        functools.partial(_ffn_kernel, base=base, next_base=None if nxt is None else nxt[1],
                          n_carry=len(carry)),
        grid=(rows // tm, f // tf),
        in_specs=in_specs,
        out_specs=out_specs,
        out_shape=out_shape,
        input_output_aliases=aliases,
        scratch_shapes=[pltpu.VMEM((tm, d), BF16)],
        compiler_params=_params(("parallel", "arbitrary")),
    )(*args)


def _matmul_kernel(a_ref, w_ref, o_ref, *, gate):
    y = _dot(a_ref[...], w_ref[...])
    o_ref[...] = (_sigmoid(y) if gate else y).astype(o_ref.dtype)


def _matmul(a, w, out_dtype, gate=False, layer=None, n=None):
    nt, k = a.shape
    n = w.shape[-1] if n is None else n
    tm = _pick_tile(nt, (1024, 512, 256, 128))
    tn = _pick_tile(n, (1024, 768, 512, 256, 128))
    if layer is None:
        w_spec = pl.BlockSpec((k, tn), lambda i, j: (0, j))
    else:
        w_spec = pl.BlockSpec((None, k, tn), lambda i, j: (layer, 0, j))
    return pl.pallas_call(
        functools.partial(_matmul_kernel, gate=gate),
        grid=(nt // tm, n // tn),
        in_specs=[pl.BlockSpec((tm, k), lambda i, j: (i, 0)), w_spec],
        out_specs=pl.BlockSpec((tm, tn), lambda i, j: (i, j)),
        out_shape=jax.ShapeDtypeStruct((nt, n), out_dtype),
        compiler_params=_params(("parallel", "parallel")),
    )(a, w)


def _mix_out_kernel(ba_ref, bh_ref, bg_ref, g_ref, wb_ref, wo_ref, x_ref, m_ref, o_ref, *, row):
    d = x_ref.shape[1]
    acc = g_ref[:, 0:d].astype(F32) * _dot(ba_ref[...], wb_ref[0])
    acc += g_ref[:, d:2 * d].astype(F32) * _dot(bh_ref[...], wb_ref[1])
    acc += g_ref[:, 2 * d:3 * d].astype(F32) * _dot(bg_ref[...], wb_ref[2])
    o_ref[...] = x_ref[...] + m_ref[row:row + 1, :] * _dot(acc.astype(BF16), wo_ref[...])


def _mix_out(br_a, br_h, br_g, gates, wb, wo, layer, x, mod, row, geo):
    nt, bw = br_a.shape
    d = x.shape[1]
    tm = _pick_tile(geo.tile_rows, (256, 128))
    cidx = functools.partial(_cond_index, tm=tm, row0=0, n_ctx_rows=geo.n_ctx_rows, seq_rows=geo.dec_seq)
    bspec = pl.BlockSpec((tm, bw), lambda i: (i, 0))
    return pl.pallas_call(
        functools.partial(_mix_out_kernel, row=row),
        grid=(nt // tm,),
        in_specs=[bspec, bspec, bspec,
                  pl.BlockSpec((tm, 3 * d), lambda i: (i, 0)),
                  pl.BlockSpec((None, 3, bw, d), lambda i: (layer, 0, 0, 0), pipeline_mode=pl.Buffered(1)),
                  pl.BlockSpec((None, d, d), lambda i: (layer, 0, 0), pipeline_mode=pl.Buffered(1)),
                  pl.BlockSpec((tm, d), lambda i: (i, 0)),
                  pl.BlockSpec((None, 9, d), lambda i: (cidx(i), 0, 0))],
        out_specs=pl.BlockSpec((tm, d), lambda i: (i, 0)),
        out_shape=jax.ShapeDtypeStruct((nt, d), F32),
        compiler_params=_params(("parallel",)),
    )(br_a, br_h, br_g, gates, wb, wo, x, mod)


def _half_mean_matrix(da):
    m = np.kron(np.eye(2), np.ones((da, da))) / da
    return jnp.asarray(m, BF16)


def _qk_normed(x, gmat, gain):
    sq = x * x
    hi = sq.astype(BF16)
    lo = (sq - hi.astype(F32)).astype(BF16)
    msq = _dot(hi, gmat) + _dot(lo, gmat)
    return x * lax.rsqrt(msq + EPS) * gain


def _rotated(x, cos, sin_signed, da):
    q = da // 4
    width = x.shape[-1]
    lane = lax.broadcasted_iota(jnp.int32, x.shape, 1)
    first = (lane % (2 * q)) < q
    partner = jnp.where(first, pltpu.roll(x, width - q, axis=1), pltpu.roll(x, q, axis=1))
    return x * cos + partner * sin_signed


def _attn_prep_kernel(*refs, rope, emit_f32, da):
    if rope:
        q_ref, k_ref, v_ref, gm_ref, gn_ref, cos_ref, sin_ref = refs[:7]
        outs = refs[7:]
    else:
        q_ref, k_ref, v_ref, gm_ref, gn_ref = refs[:5]
        outs = refs[5:]
    gm = gm_ref[...]
    w = 2 * da
    for h in range(q_ref.shape[1] // w):
        hc = slice(h * w, (h + 1) * w)
        q = _qk_normed(q_ref[:, hc], gm, gn_ref[0:1, :])
        k = _qk_normed(k_ref[:, hc], gm, gn_ref[1:2, :])
        if rope:
            q = _rotated(q, cos_ref[...], sin_ref[...], da)
            k = _rotated(k, cos_ref[...], sin_ref[...], da)
        v = v_ref[:, hc]
        outs[0][h] = (q * (math.log2(math.e) * da ** -0.5)).T.astype(BF16)
        outs[1][h] = k.astype(BF16)
        outs[2][h] = v.T.astype(BF16)
        if emit_f32:
            outs[3][h] = k
            outs[4][h] = v


def _attn_prep(proj, cols, gains, rope_tabs, nb, t, heads, da, row0, emit_f32):
    w = 2 * da
    tt = _pick_tile(t, (256, 128))
    nblk = t // tt
    rb0 = row0 // tt
    hw = heads * w
    assert all(c % hw == 0 for c in cols)
    rope = rope_tabs is not None

    def col(c0):
        return pl.BlockSpec((tt, hw), lambda b, i: (rb0 + b * nblk + i, c0 // hw))

    in_specs = [col(cols[0]), col(cols[1]), col(cols[2]),
                pl.BlockSpec((w, w), lambda b, i: (0, 0)),
                pl.BlockSpec((2, w), lambda b, i: (0, 0))]
    args = [proj, proj, proj, _half_mean_matrix(da), gains]
    if rope:
        in_specs += [pl.BlockSpec((tt, w), lambda b, i: (i, 0))] * 2
        args += list(rope_tabs)
    rows = pl.BlockSpec((None, heads, tt, w), lambda b, i: (b, 0, i, 0))
    cols_t = pl.BlockSpec((None, heads, w, tt), lambda b, i: (b, 0, 0, i))
    out_specs = [cols_t, rows, cols_t]
    out_shape = [jax.ShapeDtypeStruct((nb, heads, w, t), BF16),
                 jax.ShapeDtypeStruct((nb, heads, t, w), BF16),
                 jax.ShapeDtypeStruct((nb, heads, w, t), BF16)]
    if emit_f32:
        out_specs += [rows, rows]
        out_shape += [jax.ShapeDtypeStruct((nb, heads, t, w), F32)] * 2
    return pl.pallas_call(
        functools.partial(_attn_prep_kernel, rope=rope, emit_f32=emit_f32, da=da),
        grid=(nb, nblk),
        in_specs=in_specs,
        out_specs=out_specs,
        out_shape=out_shape,
        compiler_params=_params(("parallel", "parallel")),
    )(*args)


def _attn_kernel(qt_ref, k_ref, vt_ref, lam_ref, g_ref, o_ref, *, tk, lam_init, da):
    nk = k_ref.shape[0] // tk
    qt = qt_ref[...]
    feat = lax.broadcasted_iota(jnp.int32, qt.shape, 0)
    zero = jnp.zeros_like(qt)

    tq = qt.shape[1]
    q2t = jnp.concatenate([jnp.where(feat < da, qt, zero), jnp.where(feat >= da, qt, zero)], axis=1)

    def scores(i):
        return _dot(k_ref[i * tk:(i + 1) * tk, :], q2t)

    ahead = [scores(i) for i in range(min(ATTN_LOOKAHEAD, nk))]
    m = l = acc = None
    for i in range(nk):
        s = ahead.pop(0)
        if i + ATTN_LOOKAHEAD < nk:
            ahead.append(scores(i + ATTN_LOOKAHEAD))
        m_tile = jnp.max(s, axis=0, keepdims=True)
        m_new = m_tile if i == 0 else jnp.maximum(m, m_tile)
        p = jnp.exp2(s - m_new)
        pv = _dot(vt_ref[:, i * tk:(i + 1) * tk], p.astype(BF16))
        if i == 0:
            l, acc = jnp.sum(p, axis=0, keepdims=True), pv
        else:
            alpha = jnp.exp2(m - m_new)
            l = alpha * l + jnp.sum(p, axis=0, keepdims=True)
            acc = alpha * acc + pv
        m = m_new
    o = acc / l
    lam_p = lam_ref[...]
    lam = (jnp.exp(jnp.sum(lam_p[0:1, :] * lam_p[1:2, :], axis=-1, keepdims=True))
           - jnp.exp(jnp.sum(lam_p[2:3, :] * lam_p[3:4, :], axis=-1, keepdims=True)) + lam_init)
    o = o[:, :tq] - lam * o[:, tq:]
    o = o * lax.rsqrt(jnp.mean(o * o, axis=0, keepdims=True) + EPS) * g_ref[...]
    o_ref[...] = (o * (1.0 - lam_init)).T.astype(BF16)


def _attention(qt, k, vt, lam_p, gain, lam_init, da):
    nb, heads, w, n = qt.shape
    m = k.shape[2]
    tq = _pick_tile(n, (256, 128))
    tk = _pick_tile(m, (256, 128))
    nq = n // tq
    return pl.pallas_call(
        functools.partial(_attn_kernel, tk=tk, lam_init=lam_init, da=da),
        grid=(nb, heads, nq),
        in_specs=[pl.BlockSpec((None, None, w, tq), lambda b, h, i: (b, h, 0, i)),
                  pl.BlockSpec((None, None, m, w), lambda b, h, i: (b, h, 0, 0)),
                  pl.BlockSpec((None, None, w, m), lambda b, h, i: (b, h, 0, 0)),
                  pl.BlockSpec(lam_p.shape, lambda b, h, i: (0, 0)),
                  pl.BlockSpec((w, 1), lambda b, h, i: (0, 0))],
        out_specs=pl.BlockSpec((tq, w), lambda b, h, i: (b * nq + i, h)),
        out_shape=jax.ShapeDtypeStruct((nb * n, heads * w), BF16),
        compiler_params=_params(("parallel", "parallel", "parallel")),
    )(qt, k, vt, lam_p, gain)


def _scan_constants(c, reverse):
    i = np.arange(c)[:, None]
    t = np.arange(c)[None, :]

    levels, masks = [], []
    g = c
    while g >= 2:
        h = g // 2
        mat = np.zeros((c, c), np.float32)
        for r in range(c):
            mid = (r // g) * g + h
            if not reverse:
                if r >= mid:
                    mat[r, mid:r + 1] = 1.0
                else:
                    mat[r, r + 1:mid] = 1.0
            else:
                if r < mid:
                    mat[r, r:mid] = 1.0
                else:
                    mat[r, mid:r] = 1.0
        same = (i // g) == (t // g)
        if not reverse:
            ok = same & ((i % g) >= h) & ((t % g) < h)
        else:
            ok = same & ((i % g) < h) & ((t % g) >= h)
        levels.append(mat)
        masks.append(ok.astype(np.float32))
        g = h
    masks.append(np.eye(c, dtype=np.float32))
    return jnp.asarray(np.concatenate(levels, axis=0), BF16), jnp.asarray(np.stack(masks), F32)


def _chunk_cumsum(x, c, reverse):
    tb = x.shape[0]
    pos = lax.broadcasted_iota(jnp.int32, x.shape, 0) % c
    s = 1
    while s < c:
        if reverse:
            x = x + jnp.where(pos < c - s, pltpu.roll(x, tb - s, axis=0), 0.0)
        else:
            x = x + jnp.where(pos >= s, pltpu.roll(x, s, axis=0), 0.0)
        s *= 2
    return x


def _scan_block(q, k, v, la, slow_ref, mask_ref, st_ref, u_ref, o_ref, *, c, reverse):
    tb = q.shape[0]
    n = tb // c
    hb, dv, dk = st_ref.shape
    chunks = [slice(ci * c, (ci + 1) * c) for ci in range(n)]
    kcols = [slice(j * dk, (j + 1) * dk) for j in range(hb)]
    vcols = [slice(j * dv, (j + 1) * dv) for j in range(hb)]
    ref_row = c // 2 if reverse else c // 2 - 1
    end_row = 0 if reverse else c - 1
    b = _chunk_cumsum(la, c, reverse)
    b_ref = jnp.concatenate([jnp.broadcast_to(b[sl][ref_row:ref_row + 1], (c, b.shape[1])) for sl in chunks], axis=0)
    b_end = jnp.concatenate([jnp.broadcast_to(b[sl][end_row:end_row + 1], (c, b.shape[1])) for sl in chunks], axis=0)
    zq = b - b_ref
    safe = jnp.max(jnp.abs(zq)) < SAFE_EXPONENT
    q_in = (q * jnp.exp(b)).astype(BF16)
    k_up = (k * jnp.exp(b_end - b)).astype(BF16)
    v_bf = v.astype(BF16)
    row = lax.broadcasted_iota(jnp.int32, (c, c), 0)
    col = lax.broadcasted_iota(jnp.int32, (c, c), 1)
    visible = (col >= row) if reverse else (col <= row)
    n_levels = slow_ref.shape[0] // c

    def fast_scores():
        q_mid = (q * jnp.exp(zq)).astype(BF16)
        k_mid = (k * jnp.exp(-zq)).astype(BF16)
        return jnp.concatenate([jnp.where(visible, _dot_nt(q_mid[sl, kc], k_mid[sl, kc]), 0.0)
                                for kc in kcols for sl in chunks], axis=0)

    def slow_scores():
        q_bf = q.astype(BF16)
        k_bf = k.astype(BF16)
        parts = _split3(la)
        out = []
        for kc in kcols:
            for sl in chunks:
                zs = _dot_exact_lhs(slow_ref[...], tuple(p[sl, kc] for p in parts))
                a = mask_ref[n_levels] * _dot_nt(q_bf[sl, kc], k_bf[sl, kc])
                for lv in range(n_levels):
                    e = jnp.exp(zs[lv * c:(lv + 1) * c])
                    a += mask_ref[lv] * _dot_nt((q[sl, kc] * e).astype(BF16), (k[sl, kc] * e).astype(BF16))
                out.append(a)
        return jnp.concatenate(out, axis=0)

    a = lax.cond(safe, fast_scores, slow_scores).astype(BF16)
    for j in range(hb):
        for ci, sl in enumerate(chunks):
            u_ref[j * n + ci] = _dot_tn(v_bf[sl, vcols[j]], k_up[sl, kcols[j]])
    entry = {}
    for j in range(hb):
        st = st_ref[j]
        for ci in (range(n - 1, -1, -1) if reverse else range(n)):
            entry[j, ci] = st.astype(BF16)
            st = st * jnp.exp(b_end[chunks[ci], kcols[j]][0:1]) + u_ref[j * n + ci]
        st_ref[j] = st
    for j in range(hb):
        for ci, sl in enumerate(chunks):
            rows = slice(j * tb + ci * c, j * tb + (ci + 1) * c)
            o_ref[sl, vcols[j]] = _dot_nt(q_in[sl, kcols[j]], entry[j, ci]) + _dot(a[rows], v_bf[sl, vcols[j]])


def _scan_finish(o_ref, prev_ref, gate_ref, gn_ref, y_ref):
    dv = gn_ref.shape[1]
    o = o_ref[...] + prev_ref[...]
    gate = _silu(gate_ref[...])
    for j in range(o.shape[1] // dv):
        cols = slice(j * dv, (j + 1) * dv)
        oj = o[:, cols]
        oj = oj * lax.rsqrt(jnp.mean(oj * oj, axis=-1, keepdims=True) + EPS) * gn_ref[...]
        y_ref[:, cols] = (oj * gate[:, cols]).astype(BF16)


def _scan_state_io(s0_ref, sT_ref, st_ref, do_body):
    t = pl.program_id(2)
    hb = st_ref.shape[0]

    @pl.when(t == 0)
    def _():
        for j in range(hb):
            st_ref[j] = s0_ref[j].T

    do_body()

    @pl.when(t == pl.num_programs(2) - 1)
    def _():
        for j in range(hb):
            sT_ref[j] = st_ref[j].T


def _hgrn_kernel(*refs, c, reverse, last_pass):
    if last_pass:
        (q_ref, f_ref, i_ref, lb_ref, s0_ref, slow_ref, mask_ref, prev_ref, gate_ref, gn_ref,
         y_ref, sT_ref, st_ref, u_ref, o_scr) = refs
    else:
        (q_ref, f_ref, i_ref, lb_ref, s0_ref, slow_ref, mask_ref,
         y_ref, sT_ref, st_ref, u_ref) = refs
        o_scr = y_ref

    def body():
        lb = lb_ref[...]
        f = lb + (1.0 - lb) * _sigmoid(f_ref[...])
        _scan_block(_silu(q_ref[...]), 1.0 - f, i_ref[...], jnp.log(f),
                    slow_ref, mask_ref, st_ref, u_ref, o_scr, c=c, reverse=reverse)
        if last_pass:
            _scan_finish(o_scr, prev_ref, gate_ref, gn_ref, y_ref)

    _scan_state_io(s0_ref, sT_ref, st_ref, body)


def _gla_kernel(*refs, c, reverse, last_pass, rank, q_scale):
    if last_pass:
        (q_ref, k_ref, v_ref, r_ref, wd_ref, bd_ref, s0_ref, slow_ref, mask_ref,
         prev_ref, gate_ref, gn_ref, y_ref, sT_ref, st_ref, u_ref, o_scr) = refs
    else:
        (q_ref, k_ref, v_ref, r_ref, wd_ref, bd_ref, s0_ref, slow_ref, mask_ref,
         y_ref, sT_ref, st_ref, u_ref) = refs
        o_scr = y_ref

    def body():
        lo = rank if reverse else 0
        r = r_ref[...][:, lo:lo + rank].astype(BF16)
        logit = _dot(r, wd_ref[...].astype(BF16)) + bd_ref[...]
        la = (jnp.minimum(logit, 0.0) - jnp.log(1.0 + jnp.exp(-jnp.abs(logit)))) / GLA_TAU
        _scan_block(q_ref[...] * q_scale, k_ref[...], v_ref[...], la,
                    slow_ref, mask_ref, st_ref, u_ref, o_scr, c=c, reverse=reverse)
        if last_pass:
            _scan_finish(o_scr, prev_ref, gate_ref, gn_ref, y_ref)

    _scan_state_io(s0_ref, sT_ref, st_ref, body)


def _scan_call(kernel, proj, col_specs, extra, s0, prev, gate_col, gn, nb, t, heads, dk, dv, row0,
               reverse, **kw):
    tb = _pick_tile(t, (SCAN_BLOCK, 128))
    c = min(SCAN_CHUNK, tb)
    nblk = t // tb
    rb0 = row0 // tb
    last_pass = prev is not None
    slow, masks = _scan_constants(c, reverse)

    def tblk(i):
        return (nblk - 1 - i) if reverse else i

    hb = _pick_tile(heads, (SCAN_HEADS, 2, 1))

    def col(off, width, per_head=True):
        if not per_head:
            return pl.BlockSpec((tb, width), lambda b, h, i: (rb0 + b * nblk + tblk(i), off // width))
        assert off % (hb * width) == 0
        return pl.BlockSpec((tb, hb * width),
                            lambda b, h, i: (rb0 + b * nblk + tblk(i), off // (hb * width) + h))

    in_specs = [col(*spec[1:]) for spec in col_specs]
    args = [spec[0] for spec in col_specs]
    for e in extra:
        in_specs.append(pl.BlockSpec((e.shape[0], hb * dk), lambda b, h, i: (0, h)))
        args.append(e)
    in_specs.append(pl.BlockSpec((None, hb, dk, dv), lambda b, h, i: (b, h, 0, 0)))
    args.append(s0)
    for cst in (slow, masks):
        in_specs.append(pl.BlockSpec(cst.shape, lambda b, h, i, nd=cst.ndim: (0,) * nd))
        args.append(cst)
    scratch = [pltpu.VMEM((hb, dv, dk), F32), pltpu.VMEM((hb * (tb // c), dv, dk), F32)]
    if last_pass:
        in_specs += [pl.BlockSpec((tb, hb * dv), lambda b, h, i: (b * nblk + tblk(i), h)),
                     col(gate_col, dv),
                     pl.BlockSpec((1, dv), lambda b, h, i: (0, 0))]
        args += [prev, proj, gn]
        scratch.append(pltpu.VMEM((tb, hb * dv), F32))
    y_dtype = BF16 if last_pass else F32
    return pl.pallas_call(
        functools.partial(kernel, c=c, reverse=reverse, last_pass=last_pass, **kw),
        grid=(nb, heads // hb, nblk),
        in_specs=in_specs,
        out_specs=[pl.BlockSpec((tb, hb * dv), lambda b, h, i: (b * nblk + tblk(i), h)),
                   pl.BlockSpec((None, hb, dk, dv), lambda b, h, i: (b, h, 0, 0))],
        out_shape=[jax.ShapeDtypeStruct((nb * t, heads * dv), y_dtype),
                   jax.ShapeDtypeStruct((nb, heads, dk, dv), F32)],
        scratch_shapes=scratch,
        compiler_params=_params(("parallel", "parallel", "arbitrary")),
    )(*args)


class _Geometry:
    pass


def _rope_tables(n_tok, da):
    ax = da // 2
    rows = n_tok // GRID_W
    r = jnp.broadcast_to(jnp.arange(rows, dtype=F32)[:, None], (rows, GRID_W)).reshape(-1)
    col = jnp.broadcast_to(jnp.arange(GRID_W, dtype=F32)[None, :], (rows, GRID_W)).reshape(-1)
    inv = ROPE_BASE ** (-jnp.arange(0, ax, 2, dtype=F32) / ax)
    ar, ac = r[:, None] * inv, col[:, None] * inv
    ang = jnp.concatenate([ar, ar, ac, ac], axis=-1)
    cos, sin = jnp.cos(ang), jnp.sin(ang)
    q = da // 4
    sign = jnp.where((jnp.arange(da) % (2 * q)) < q, -1.0, 1.0).astype(F32)
    return jnp.tile(cos, (1, 2)), jnp.tile(sin * sign, (1, 2))


def kernel(x_prompt, x_sample, c, cache_attn_k, cache_attn_v, state_hgrn, state_gla, c_ctx, ada_w, ada_b, norm_w, ffn_w_gate, ffn_w_up, ffn_w_down, w_in, attn_qk_norm, attn_lambda, attn_out_norm, hgrn_lb_logits, hgrn_out_norm, gla_w_decay, gla_b_decay, gla_out_norm, w_branch, w_out):
    batch, seq, d = x_prompt.shape
    dec_batch, dec_seq, _ = x_sample.shape
    depth = ada_w.shape[0]
    ha, da = cache_attn_k.shape[2], cache_attn_k.shape[5]
    dva = 2 * da
    hh, hdk, hdv = state_hgrn.shape[3:]
    hg, gdk, gdv = state_gla.shape[3:]
    rank = gla_w_decay.shape[2]
    sizes = (ha * 2 * da, ha * 2 * da, ha * dva, hh * hdk, hh * hdk, hh * hdk, hh * hdv, hh * hdv,
             hg * gdk, hg * gdk, hg * gdv, hg * gdv, 2 * rank, 3 * d)
    assert sum(sizes) == w_in.shape[2] and hdk == LANES and hdv == LANES and gdk == LANES and dva == LANES

    geo = _Geometry()
    geo.n_ctx_rows = batch * seq
    geo.dec_seq = dec_seq
    geo.tile_rows = math.gcd(geo.n_ctx_rows, dec_seq)
    n_ctx = geo.n_ctx_rows

    src = [0] + [int(v) for v in np.cumsum(sizes)]
    n_main = src[12]
    names = ('aq', 'ak', 'av', 'hq', 'hf_f', 'hf_b', 'hi', 'hg', 'gq', 'gk', 'gv', 'gg')
    off = {nm: src[i] for i, nm in enumerate(names)}

    n_cond = 1 + dec_batch
    cond = jnp.concatenate([c_ctx[None], c, jnp.zeros((-n_cond % 8, d), F32)], axis=0)
    mod = _adaln(cond, ada_w, ada_b).reshape(depth, cond.shape[0], 9, d)

    lb_w = jax.nn.softmax(hgrn_lb_logits.astype(F32), axis=0)
    lower_bounds = jnp.cumsum(lb_w, axis=0) - lb_w[0]
    rope_tabs = _rope_tables(dec_seq, da)

    n_lat = dec_batch * dec_seq
    nt = n_ctx + n_lat
    zeros_h = jnp.zeros((batch, hh, hdk, hdv), F32)
    zeros_g = jnp.zeros((batch, hg, gdk, gdv), F32)
    ks_l, vs_l, sh_l, sg_l = [], [], [], []
    x = None
    wg_bf, wu_bf, wd_bf = ffn_w_gate.astype(BF16), ffn_w_up.astype(BF16), ffn_w_down.astype(BF16)
    w_in_bf, wb_bf, wo_bf = w_in.astype(BF16), w_branch.astype(BF16), w_out.astype(BF16)

    for l in range(depth):
        m_l = mod[l]
        nw = norm_w[l]
        ffn1 = (m_l, nw[0:1], wg_bf, wu_bf, wd_bf, (l, 0), 0, geo)
        if l == 0:
            part = _ffn(x_prompt.reshape(n_ctx, d), *ffn1, out_rows=nt, nxt=(nw[1:2], 3))
            x, h_mix = _ffn(x_sample.reshape(n_lat, d), *ffn1, token_row0=n_ctx, out_row0=n_ctx,
                            out_rows=nt, carry=part, nxt=(nw[1:2], 3))
        else:
            x, h_mix = _ffn(x, *ffn1, nxt=(nw[1:2], 3))
        proj = _matmul(h_mix, w_in_bf, F32, layer=l, n=n_main)
        w_glr = jnp.pad(w_in_bf[l, :, src[12]:src[13]], ((0, 0), (0, LANES - 2 * rank)))
        glr = _matmul(h_mix, w_glr, F32)
        gates = _matmul(h_mix, w_in_bf[l, :, src[13]:], BF16, gate=True)

        lam_init = 0.8 - 0.6 * math.exp(-0.3 * l)
        gains = jnp.tile(attn_qk_norm[l], (1, 2))
        cols = (off['aq'], off['ak'], off['av'])
        gn_a = attn_out_norm[l][:, None]
        qp, kp, vp, k32, v32 = _attn_prep(proj, cols, gains, None, batch, seq, ha, da, 0, True)
        ks_l.append(k32.reshape(batch, ha, seq, 2, da))
        vs_l.append(v32)
        qs, ksm, vsm = _attn_prep(proj, cols, gains, rope_tabs, dec_batch, dec_seq, ha, da, n_ctx, False)
        k_all = jnp.concatenate([cache_attn_k[:, l].reshape(dec_batch, ha, -1, 2 * da).astype(BF16), ksm], axis=2)
        v_all = jnp.concatenate([jnp.swapaxes(cache_attn_v[:, l], 2, 3).astype(BF16), vsm], axis=3)
        br_a = jnp.concatenate([_attention(qp, kp, vp, attn_lambda[l], gn_a, lam_init, da),
                                _attention(qs, k_all, v_all, attn_lambda[l], gn_a, lam_init, da)], axis=0)

        lb = lower_bounds[l][:, None]
        gn_h = hgrn_out_norm[l][None]

        def hgrn(nb, t, row0, s0):
            specs_b = ((proj, off['hq'], hdk), (proj, off['hf_b'], hdk), (proj, off['hi'], hdv))
            specs_f = ((proj, off['hq'], hdk), (proj, off['hf_f'], hdk), (proj, off['hi'], hdv))
            o_b, s_b = _scan_call(_hgrn_kernel, proj, specs_b, [lb[1]], s0[1], None, None, None,
                                  nb, t, hh, hdk, hdv, row0, True)
            y, s_f = _scan_call(_hgrn_kernel, proj, specs_f, [lb[0]], s0[0], o_b, off['hg'], gn_h,
                                nb, t, hh, hdk, hdv, row0, False)
            return y, jnp.stack([s_f, s_b], axis=1)

        y_p, s_h = hgrn(batch, seq, 0, (zeros_h, zeros_h))
        y_s, _ = hgrn(dec_batch, dec_seq, n_ctx, (state_hgrn[:, l, 0], state_hgrn[:, l, 1]))
        br_h = jnp.concatenate([y_p, y_s], axis=0)
        sh_l.append(s_h)

        wd = gla_w_decay[l]
        bd = gla_b_decay[l][:, None]
        gn_g = gla_out_norm[l][None]

        def gla(nb, t, row0, s0):
            specs = ((proj, off['gq'], gdk), (proj, off['gk'], gdk), (proj, off['gv'], gdv),
                     (glr, 0, LANES, False))
            kw = dict(rank=rank, q_scale=gdk ** -0.5)
            o_b, s_b = _scan_call(_gla_kernel, proj, specs, [wd[1], bd[1]], s0[1], None, None, None,
                                  nb, t, hg, gdk, gdv, row0, True, **kw)
            y, s_f = _scan_call(_gla_kernel, proj, specs, [wd[0], bd[0]], s0[0], o_b, off['gg'], gn_g,
                                nb, t, hg, gdk, gdv, row0, False, **kw)
            return y, jnp.stack([s_f, s_b], axis=1)

        y_p, s_g = gla(batch, seq, 0, (zeros_g, zeros_g))
        y_s, _ = gla(dec_batch, dec_seq, n_ctx, (state_gla[:, l, 0], state_gla[:, l, 1]))
        br_g = jnp.concatenate([y_p, y_s], axis=0)
        sg_l.append(s_g)

        x = _mix_out(br_a, br_h, br_g, gates, wb_bf, wo_bf, l, x, m_l, 5, geo)
        ffn2 = (m_l, nw[2:3], wg_bf, wu_bf, wd_bf, (l, 1), 6, geo)
        if l + 1 < depth:
            x = _ffn(x, *ffn2)[0]
        else:
            y_prompt = _ffn(x, *ffn2, rows=n_ctx)[0].reshape(batch, seq, d)
            y_sample = _ffn(x, *ffn2, rows=n_lat, in_row0=n_ctx,
                            token_row0=n_ctx)[0].reshape(dec_batch, dec_seq, d)

    return (y_prompt, y_sample, jnp.stack(ks_l, axis=1), jnp.stack(vs_l, axis=1),
            jnp.stack(sh_l, axis=1), jnp.stack(sg_l, axis=1))
```

```python
import functools
import math

import jax
import jax.numpy as jnp
import numpy as np
from jax import lax
from jax.experimental import pallas as pl
from jax.experimental.pallas import tpu as pltpu

F32 = jnp.float32
BF16 = jnp.bfloat16

GRID_W = 64
GLA_TAU = 16.0
ROPE_BASE = 10000.0
EPS = 1e-6
LANES = 128
SCAN_CHUNK = 64
SCAN_BLOCK = 256
SCAN_HEADS = 4
ATTN_LOOKAHEAD = 2
SAFE_EXPONENT = 80.0
VMEM_LIMIT = 56 * 1024 * 1024


def _dot(a, b):
    return jnp.dot(a, b, preferred_element_type=F32)


def _dot_nt(a, b):
    return lax.dot_general(a, b, (((1,), (1,)), ((), ())), preferred_element_type=F32)


def _dot_tn(a, b):
    return lax.dot_general(a, b, (((0,), (0,)), ((), ())), preferred_element_type=F32)


def _sigmoid(x):
    return 1.0 / (1.0 + jnp.exp(-x))


def _silu(x):
    return x * _sigmoid(x)


def _split3(x):
    hi = x.astype(BF16)
    r1 = x - hi.astype(F32)
    mid = r1.astype(BF16)
    lo = (r1 - mid.astype(F32)).astype(BF16)
    return hi, mid, lo


def _dot_exact_lhs(m, parts):
    return _dot(m, parts[0]) + _dot(m, parts[1]) + _dot(m, parts[2])


def _pick_tile(n, candidates):
    for c in candidates:
        if n % c == 0:
            return c
    return n


def _params(sem):
    return pltpu.CompilerParams(dimension_semantics=sem, vmem_limit_bytes=VMEM_LIMIT)


def _adaln_kernel(c_ref, w_ref, b_ref, o_ref):
    s = _silu(c_ref[...]).astype(BF16)
    o_ref[...] = _dot(s, w_ref[...].astype(BF16)) + b_ref[...]


def _adaln(cond, ada_w, ada_b):
    depth, d, n = ada_w.shape
    rows = cond.shape[0]
    tn = _pick_tile(n, (1024, 512, 256, 128))
    return pl.pallas_call(
        _adaln_kernel,
        grid=(depth, n // tn),
        in_specs=[pl.BlockSpec((rows, d), lambda l, j: (0, 0)),
                  pl.BlockSpec((None, d, tn), lambda l, j: (l, 0, j)),
                  pl.BlockSpec((None, 1, tn), lambda l, j: (l, 0, j))],
        out_specs=pl.BlockSpec((None, rows, tn), lambda l, j: (l, 0, j)),
        out_shape=jax.ShapeDtypeStruct((depth, rows, n), F32),
        compiler_params=_params(("parallel", "parallel")),
    )(cond, ada_w, ada_b.reshape(depth, 1, n))


def _modulated(x, nw, shift, scale):
    y = x * lax.rsqrt(jnp.mean(x * x, axis=-1, keepdims=True) + EPS) * nw
    return y * (1.0 + scale) + shift


def _ffn_kernel(*refs, base, next_base, n_carry):
    x_ref, m_ref, nw_ref, wg_ref, wu_ref, wd_ref = refs[:6]
    pos = 6
    if next_base is not None:
        nw_next_ref = refs[pos]
        pos += 1
    pos += n_carry
    o_ref = refs[pos]
    if next_base is not None:
        h_next_ref = refs[pos + 1]
    h_ref = refs[-1]
    j = pl.program_id(1)

    @pl.when(j == 0)
    def _():
        h = _modulated(x_ref[...], nw_ref[...], m_ref[base:base + 1, :], m_ref[base + 1:base + 2, :])
        h_ref[...] = h.astype(BF16)
        o_ref[...] = jnp.zeros_like(o_ref)

    h = h_ref[...]
    g = _dot(h, wg_ref[...])
    u = _dot(h, wu_ref[...])
    o_ref[...] += _dot((_silu(g) * u).astype(BF16), wd_ref[...])

    @pl.when(j == pl.num_programs(1) - 1)
    def _():
        y = x_ref[...] + 0.5 * m_ref[base + 2:base + 3, :] * o_ref[...]
        o_ref[...] = y
        if next_base is not None:
            h_next_ref[...] = _modulated(y, nw_next_ref[...], m_ref[next_base:next_base + 1, :],
                                         m_ref[next_base + 1:next_base + 2, :]).astype(BF16)


def _cond_index(i, tm, row0, n_ctx_rows, seq_rows):
    r = row0 + i * tm
    return jnp.where(r < n_ctx_rows, 0, (r - n_ctx_rows) // seq_rows + 1)


def _ffn(x, mod, nw, wg, wu, wd, wsel, base, geo, *, rows=None, in_row0=0, token_row0=0, out_row0=0,
         out_rows=None, carry=(), nxt=None):
    d = x.shape[1]
    rows = x.shape[0] if rows is None else rows
    out_rows = rows if out_rows is None else out_rows
    f = wg.shape[-1]
    tm = _pick_tile(geo.tile_rows, (512, 256, 128))
    tf = _pick_tile(f, (512, 256, 128))
    cidx = functools.partial(_cond_index, tm=tm, row0=token_row0, n_ctx_rows=geo.n_ctx_rows,
                             seq_rows=geo.dec_seq)
    ib, ob = in_row0 // tm, out_row0 // tm
    l, s = wsel
    in_specs = [pl.BlockSpec((tm, d), lambda i, j: (i + ib, 0)),
                pl.BlockSpec((None, 9, d), lambda i, j: (cidx(i), 0, 0)),
                pl.BlockSpec((1, d), lambda i, j: (0, 0)),
                pl.BlockSpec((None, None, d, tf), lambda i, j: (l, s, 0, j)),
                pl.BlockSpec((None, None, d, tf), lambda i, j: (l, s, 0, j)),
                pl.BlockSpec((None, None, tf, d), lambda i, j: (l, s, j, 0))]
    args = [x, mod, nw, wg, wu, wd]
    out_specs = [pl.BlockSpec((tm, d), lambda i, j: (i + ob, 0))]
    out_shape = [jax.ShapeDtypeStruct((out_rows, d), F32)]
    if nxt is not None:
        in_specs.append(pl.BlockSpec((1, d), lambda i, j: (0, 0)))
        args.append(nxt[0])
        out_specs.append(pl.BlockSpec((tm, d), lambda i, j: (i + ob, 0)))
        out_shape.append(jax.ShapeDtypeStruct((out_rows, d), BF16))
    aliases = {len(args) + k: k for k in range(len(carry))}
    in_specs += [pl.BlockSpec(memory_space=pl.ANY)] * len(carry)
    args += list(carry)
    return pl.pallas_call(
        functools.partial(_ffn_kernel, base=base, next_base=None if nxt is None else nxt[1],
                          n_carry=len(carry)),
        grid=(rows // tm, f // tf),
        in_specs=in_specs,
        out_specs=out_specs,
        out_shape=out_shape,
        input_output_aliases=aliases,
        scratch_shapes=[pltpu.VMEM((tm, d), BF16)],
        compiler_params=_params(("parallel", "arbitrary")),
    )(*args)


def _matmul_kernel(a_ref, w_ref, o_ref, *, gate):
    y = _dot(a_ref[...], w_ref[...])
    o_ref[...] = (_sigmoid(y) if gate else y).astype(o_ref.dtype)


def _matmul(a, w, out_dtype, gate=False, layer=None, n=None):
    nt, k = a.shape
    n = w.shape[-1] if n is None else n
    tm = _pick_tile(nt, (1024, 512, 256, 128))
    tn = _pick_tile(n, (1024, 768, 512, 256, 128))
    if layer is None:
        w_spec = pl.BlockSpec((k, tn), lambda i, j: (0, j))
    else:
        w_spec = pl.BlockSpec((None, k, tn), lambda i, j: (layer, 0, j))
    return pl.pallas_call(
        functools.partial(_matmul_kernel, gate=gate),
        grid=(nt // tm, n // tn),
        in_specs=[pl.BlockSpec((tm, k), lambda i, j: (i, 0)), w_spec],
        out_specs=pl.BlockSpec((tm, tn), lambda i, j: (i, j)),
        out_shape=jax.ShapeDtypeStruct((nt, n), out_dtype),
        compiler_params=_params(("parallel", "parallel")),
    )(a, w)


def _mix_out_kernel(ba_ref, bh_ref, bg_ref, g_ref, wb_ref, wo_ref, x_ref, m_ref, o_ref, *, row):
    d = x_ref.shape[1]
    acc = g_ref[:, 0:d].astype(F32) * _dot(ba_ref[...], wb_ref[0])
    acc += g_ref[:, d:2 * d].astype(F32) * _dot(bh_ref[...], wb_ref[1])
    acc += g_ref[:, 2 * d:3 * d].astype(F32) * _dot(bg_ref[...], wb_ref[2])
    o_ref[...] = x_ref[...] + m_ref[row:row + 1, :] * _dot(acc.astype(BF16), wo_ref[...])


def _mix_out(br_a, br_h, br_g, gates, wb, wo, layer, x, mod, row, geo):
    nt, bw = br_a.shape
    d = x.shape[1]
    tm = _pick_tile(geo.tile_rows, (256, 128))
    cidx = functools.partial(_cond_index, tm=tm, row0=0, n_ctx_rows=geo.n_ctx_rows, seq_rows=geo.dec_seq)
    bspec = pl.BlockSpec((tm, bw), lambda i: (i, 0))
    return pl.pallas_call(
        functools.partial(_mix_out_kernel, row=row),
        grid=(nt // tm,),
        in_specs=[bspec, bspec, bspec,
                  pl.BlockSpec((tm, 3 * d), lambda i: (i, 0)),
                  pl.BlockSpec((None, 3, bw, d), lambda i: (layer, 0, 0, 0), pipeline_mode=pl.Buffered(1)),
                  pl.BlockSpec((None, d, d), lambda i: (layer, 0, 0), pipeline_mode=pl.Buffered(1)),
                  pl.BlockSpec((tm, d), lambda i: (i, 0)),
                  pl.BlockSpec((None, 9, d), lambda i: (cidx(i), 0, 0))],
        out_specs=pl.BlockSpec((tm, d), lambda i: (i, 0)),
        out_shape=jax.ShapeDtypeStruct((nt, d), F32),
        compiler_params=_params(("parallel",)),
    )(br_a, br_h, br_g, gates, wb, wo, x, mod)


def _half_mean_matrix(da):
    m = np.kron(np.eye(2), np.ones((da, da))) / da
    return jnp.asarray(m, BF16)


def _qk_normed(x, gmat, gain):
    sq = x * x
    hi = sq.astype(BF16)
    lo = (sq - hi.astype(F32)).astype(BF16)
    msq = _dot(hi, gmat) + _dot(lo, gmat)
    return x * lax.rsqrt(msq + EPS) * gain


def _rotated(x, cos, sin_signed, da):
    q = da // 4
    width = x.shape[-1]
    lane = lax.broadcasted_iota(jnp.int32, x.shape, 1)
    first = (lane % (2 * q)) < q
    partner = jnp.where(first, pltpu.roll(x, width - q, axis=1), pltpu.roll(x, q, axis=1))
    return x * cos + partner * sin_signed


def _attn_prep_kernel(*refs, rope, emit_f32, da):
    if rope:
        q_ref, k_ref, v_ref, gm_ref, gn_ref, cos_ref, sin_ref = refs[:7]
        outs = refs[7:]
    else:
        q_ref, k_ref, v_ref, gm_ref, gn_ref = refs[:5]
        outs = refs[5:]
    gm = gm_ref[...]
    w = 2 * da
    for h in range(q_ref.shape[1] // w):
        hc = slice(h * w, (h + 1) * w)
        q = _qk_normed(q_ref[:, hc], gm, gn_ref[0:1, :])
        k = _qk_normed(k_ref[:, hc], gm, gn_ref[1:2, :])
        if rope:
            q = _rotated(q, cos_ref[...], sin_ref[...], da)
            k = _rotated(k, cos_ref[...], sin_ref[...], da)
        v = v_ref[:, hc]
        outs[0][h] = (q * (math.log2(math.e) * da ** -0.5)).T.astype(BF16)
        outs[1][h] = k.astype(BF16)
        outs[2][h] = v.T.astype(BF16)
        if emit_f32:
            outs[3][h] = k
            outs[4][h] = v


def _attn_prep(proj, cols, gains, rope_tabs, nb, t, heads, da, row0, emit_f32):
    w = 2 * da
    tt = _pick_tile(t, (256, 128))
    nblk = t // tt
    rb0 = row0 // tt
    hw = heads * w
    assert all(c % hw == 0 for c in cols)
    rope = rope_tabs is not None

    def col(c0):
        return pl.BlockSpec((tt, hw), lambda b, i: (rb0 + b * nblk + i, c0 // hw))

    in_specs = [col(cols[0]), col(cols[1]), col(cols[2]),
                pl.BlockSpec((w, w), lambda b, i: (0, 0)),
                pl.BlockSpec((2, w), lambda b, i: (0, 0))]
    args = [proj, proj, proj, _half_mean_matrix(da), gains]
    if rope:
        in_specs += [pl.BlockSpec((tt, w), lambda b, i: (i, 0))] * 2
        args += list(rope_tabs)
    rows = pl.BlockSpec((None, heads, tt, w), lambda b, i: (b, 0, i, 0))
    cols_t = pl.BlockSpec((None, heads, w, tt), lambda b, i: (b, 0, 0, i))
    out_specs = [cols_t, rows, cols_t]
    out_shape = [jax.ShapeDtypeStruct((nb, heads, w, t), BF16),
                 jax.ShapeDtypeStruct((nb, heads, t, w), BF16),
                 jax.ShapeDtypeStruct((nb, heads, w, t), BF16)]
    if emit_f32:
        out_specs += [rows, rows]
        out_shape += [jax.ShapeDtypeStruct((nb, heads, t, w), F32)] * 2
    return pl.pallas_call(
        functools.partial(_attn_prep_kernel, rope=rope, emit_f32=emit_f32, da=da),
        grid=(nb, nblk),
        in_specs=in_specs,
        out_specs=out_specs,
        out_shape=out_shape,
        compiler_params=_params(("parallel", "parallel")),
    )(*args)


def _attn_kernel(*refs, tk, lam_init, da, has_ctx):
    if has_ctx:
        qt_ref, k_ref, vt_ref, ck_ref, cv_ref, lam_ref, g_ref = refs[:7]
        n_ctx = ck_ref.shape[0] // tk
    else:
        qt_ref, k_ref, vt_ref, lam_ref, g_ref = refs[:5]
        n_ctx = 0
    o_ref = refs[-1]
    nk = n_ctx + k_ref.shape[0] // tk
    qt = qt_ref[...]
    feat = lax.broadcasted_iota(jnp.int32, qt.shape, 0)
    zero = jnp.zeros_like(qt)

    tq = qt.shape[1]
    q2t = jnp.concatenate([jnp.where(feat < da, qt, zero), jnp.where(feat >= da, qt, zero)], axis=1)

    def scores(i):
        if i < n_ctx:
            k_tile = ck_ref[i * tk:(i + 1) * tk, :].astype(BF16)
        else:
            k_tile = k_ref[(i - n_ctx) * tk:(i - n_ctx + 1) * tk, :]
        return _dot(k_tile, q2t)

    def values_t(i):
        if i < n_ctx:
            return cv_ref[i * tk:(i + 1) * tk, :].T.astype(BF16)
        return vt_ref[:, (i - n_ctx) * tk:(i - n_ctx + 1) * tk]

    ahead = [scores(i) for i in range(min(ATTN_LOOKAHEAD, nk))]
    m = l = acc = None
    for i in range(nk):
        s = ahead.pop(0)
        if i + ATTN_LOOKAHEAD < nk:
            ahead.append(scores(i + ATTN_LOOKAHEAD))
        m_tile = jnp.max(s, axis=0, keepdims=True)
        m_new = m_tile if i == 0 else jnp.maximum(m, m_tile)
        p = jnp.exp2(s - m_new)
        pv = _dot(values_t(i), p.astype(BF16))
        if i == 0:
            l, acc = jnp.sum(p, axis=0, keepdims=True), pv
        else:
            alpha = jnp.exp2(m - m_new)
            l = alpha * l + jnp.sum(p, axis=0, keepdims=True)
            acc = alpha * acc + pv
        m = m_new
    o = acc / l
    lam_p = lam_ref[...]
    lam = (jnp.exp(jnp.sum(lam_p[0:1, :] * lam_p[1:2, :], axis=-1, keepdims=True))
           - jnp.exp(jnp.sum(lam_p[2:3, :] * lam_p[3:4, :], axis=-1, keepdims=True)) + lam_init)
    o = o[:, :tq] - lam * o[:, tq:]
    o = o * lax.rsqrt(jnp.mean(o * o, axis=0, keepdims=True) + EPS) * g_ref[...]
    o_ref[...] = (o * (1.0 - lam_init)).T.astype(BF16)


def _attention(qt, k, vt, lam_p, gain, lam_init, da, *, ctx=None, out_row0=0, out_rows=None, carry=None):
    nb, heads, w, n = qt.shape
    m = k.shape[2]
    tq = _pick_tile(n, (512, 256, 128))
    tk = _pick_tile(math.gcd(m, ctx[0].shape[3]) if ctx else m, (256, 128))
    nq = n // tq
    orb = out_row0 // tq
    out_rows = nb * n if out_rows is None else out_rows
    in_specs = [pl.BlockSpec((None, None, w, tq), lambda b, h, i: (b, h, 0, i)),
                pl.BlockSpec((None, None, m, w), lambda b, h, i: (b, h, 0, 0)),
                pl.BlockSpec((None, None, w, m), lambda b, h, i: (b, h, 0, 0))]
    args = [qt, k, vt]
    if ctx:
        ck, cv, layer = ctx
        spec = pl.BlockSpec((None, None, None) + ck.shape[3:], lambda b, h, i: (b, layer, h, 0, 0))
        in_specs += [spec, spec]
        args += [ck, cv]
    in_specs += [pl.BlockSpec(lam_p.shape, lambda b, h, i: (0, 0)),
                 pl.BlockSpec((w, 1), lambda b, h, i: (0, 0))]
    args += [lam_p, gain]
    aliases = {}
    if carry is not None:
        aliases = {len(args): 0}
        in_specs.append(pl.BlockSpec(memory_space=pl.ANY))
        args.append(carry)
    return pl.pallas_call(
        functools.partial(_attn_kernel, tk=tk, lam_init=lam_init, da=da, has_ctx=bool(ctx)),
        grid=(nb, heads, nq),
        in_specs=in_specs,
        out_specs=pl.BlockSpec((tq, w), lambda b, h, i: (orb + b * nq + i, h)),
        out_shape=jax.ShapeDtypeStruct((out_rows, heads * w), BF16),
        input_output_aliases=aliases,
        compiler_params=_params(("parallel", "parallel", "parallel")),
    )(*args)


def _scan_constants(c, reverse):
    i = np.arange(c)[:, None]
    t = np.arange(c)[None, :]

    levels, masks = [], []
    g = c
    while g >= 2:
        h = g // 2
        mat = np.zeros((c, c), np.float32)
        for r in range(c):
            mid = (r // g) * g + h
            if not reverse:
                if r >= mid:
                    mat[r, mid:r + 1] = 1.0
                else:
                    mat[r, r + 1:mid] = 1.0
            else:
                if r < mid:
                    mat[r, r:mid] = 1.0
                else:
                    mat[r, mid:r] = 1.0
        same = (i // g) == (t // g)
        if not reverse:
            ok = same & ((i % g) >= h) & ((t % g) < h)
        else:
            ok = same & ((i % g) < h) & ((t % g) >= h)
        levels.append(mat)
        masks.append(ok.astype(np.float32))
        g = h
    masks.append(np.eye(c, dtype=np.float32))
    return jnp.asarray(np.concatenate(levels, axis=0), BF16), jnp.asarray(np.stack(masks), F32)


def _chunk_cumsum(x, c, reverse):
    tb = x.shape[0]
    pos = lax.broadcasted_iota(jnp.int32, x.shape, 0) % c
    s = 1
    while s < c:
        if reverse:
            x = x + jnp.where(pos < c - s, pltpu.roll(x, tb - s, axis=0), 0.0)
        else:
            x = x + jnp.where(pos >= s, pltpu.roll(x, s, axis=0), 0.0)
        s *= 2
    return x


def _scan_block(q, k, v, la, slow_ref, mask_ref, st_ref, u_ref, o_ref, *, c, reverse):
    tb = q.shape[0]
    n = tb // c
    hb, dv, dk = st_ref.shape
    chunks = [slice(ci * c, (ci + 1) * c) for ci in range(n)]
    kcols = [slice(j * dk, (j + 1) * dk) for j in range(hb)]
    vcols = [slice(j * dv, (j + 1) * dv) for j in range(hb)]
    ref_row = c // 2 if reverse else c // 2 - 1
    end_row = 0 if reverse else c - 1
    b = _chunk_cumsum(la, c, reverse)
    b_ref = jnp.concatenate([jnp.broadcast_to(b[sl][ref_row:ref_row + 1], (c, b.shape[1])) for sl in chunks], axis=0)
    b_end = jnp.concatenate([jnp.broadcast_to(b[sl][end_row:end_row + 1], (c, b.shape[1])) for sl in chunks], axis=0)
    zq = b - b_ref
    safe = jnp.max(jnp.abs(zq)) < SAFE_EXPONENT
    q_in = (q * jnp.exp(b)).astype(BF16)
    k_up = (k * jnp.exp(b_end - b)).astype(BF16)
    v_bf = v.astype(BF16)
    row = lax.broadcasted_iota(jnp.int32, (c, c), 0)
    col = lax.broadcasted_iota(jnp.int32, (c, c), 1)
    visible = (col >= row) if reverse else (col <= row)
    n_levels = slow_ref.shape[0] // c

    def fast_scores():
        q_mid = (q * jnp.exp(zq)).astype(BF16)
        k_mid = (k * jnp.exp(-zq)).astype(BF16)
        return jnp.concatenate([jnp.where(visible, _dot_nt(q_mid[sl, kc], k_mid[sl, kc]), 0.0)
                                for kc in kcols for sl in chunks], axis=0)

    def slow_scores():
        q_bf = q.astype(BF16)
        k_bf = k.astype(BF16)
        parts = _split3(la)
        out = []
        for kc in kcols:
            for sl in chunks:
                zs = _dot_exact_lhs(slow_ref[...], tuple(p[sl, kc] for p in parts))
                a = mask_ref[n_levels] * _dot_nt(q_bf[sl, kc], k_bf[sl, kc])
                for lv in range(n_levels):
                    e = jnp.exp(zs[lv * c:(lv + 1) * c])
                    a += mask_ref[lv] * _dot_nt((q[sl, kc] * e).astype(BF16), (k[sl, kc] * e).astype(BF16))
                out.append(a)
        return jnp.concatenate(out, axis=0)

    a = lax.cond(safe, fast_scores, slow_scores).astype(BF16)
    for j in range(hb):
        for ci, sl in enumerate(chunks):
            u_ref[j * n + ci] = _dot_tn(v_bf[sl, vcols[j]], k_up[sl, kcols[j]])
    entry = {}
    for j in range(hb):
        st = st_ref[j]
        for ci in (range(n - 1, -1, -1) if reverse else range(n)):
            entry[j, ci] = st.astype(BF16)
            st = st * jnp.exp(b_end[chunks[ci], kcols[j]][0:1]) + u_ref[j * n + ci]
        st_ref[j] = st
    for j in range(hb):
        for ci, sl in enumerate(chunks):
            rows = slice(j * tb + ci * c, j * tb + (ci + 1) * c)
            o_ref[sl, vcols[j]] = _dot_nt(q_in[sl, kcols[j]], entry[j, ci]) + _dot(a[rows], v_bf[sl, vcols[j]])


def _scan_finish(o_ref, prev_ref, gate_ref, gn_ref, y_ref):
    dv = gn_ref.shape[1]
    o = o_ref[...] + prev_ref[...]
    gate = _silu(gate_ref[...])
    for j in range(o.shape[1] // dv):
        cols = slice(j * dv, (j + 1) * dv)
        oj = o[:, cols]
        oj = oj * lax.rsqrt(jnp.mean(oj * oj, axis=-1, keepdims=True) + EPS) * gn_ref[...]
        y_ref[:, cols] = (oj * gate[:, cols]).astype(BF16)


def _scan_state_io(s0_ref, sT_ref, st_ref, do_body):
    t = pl.program_id(2)
    hb = st_ref.shape[0]

    @pl.when(t == 0)
    def _():
        for j in range(hb):
            st_ref[j] = s0_ref[j].T

    do_body()

    @pl.when(t == pl.num_programs(2) - 1)
    def _():
        for j in range(hb):
            sT_ref[j] = st_ref[j].T


def _without(refs, pos):
    return refs if pos is None else refs[:pos] + refs[pos + 1:]


def _hgrn_kernel(*refs, c, reverse, last_pass, carry_pos):
    refs = _without(refs, carry_pos)
    if last_pass:
        (q_ref, f_ref, i_ref, lb_ref, s0_ref, slow_ref, mask_ref, prev_ref, gate_ref, gn_ref,
         y_ref, sT_ref, st_ref, u_ref, o_scr) = refs
    else:
        (q_ref, f_ref, i_ref, lb_ref, s0_ref, slow_ref, mask_ref,
         y_ref, sT_ref, st_ref, u_ref) = refs
        o_scr = y_ref

    def body():
        lb = lb_ref[...]
        f = lb + (1.0 - lb) * _sigmoid(f_ref[...])
        _scan_block(_silu(q_ref[...]), 1.0 - f, i_ref[...], jnp.log(f),
                    slow_ref, mask_ref, st_ref, u_ref, o_scr, c=c, reverse=reverse)
        if last_pass:
            _scan_finish(o_scr, prev_ref, gate_ref, gn_ref, y_ref)

    _scan_state_io(s0_ref, sT_ref, st_ref, body)


def _gla_kernel(*refs, c, reverse, last_pass, carry_pos, rank, q_scale):
    refs = _without(refs, carry_pos)
    if last_pass:
        (q_ref, k_ref, v_ref, r_ref, wd_ref, bd_ref, s0_ref, slow_ref, mask_ref,
         prev_ref, gate_ref, gn_ref, y_ref, sT_ref, st_ref, u_ref, o_scr) = refs
    else:
        (q_ref, k_ref, v_ref, r_ref, wd_ref, bd_ref, s0_ref, slow_ref, mask_ref,
         y_ref, sT_ref, st_ref, u_ref) = refs
        o_scr = y_ref

    def body():
        lo = rank if reverse else 0
        r = r_ref[...][:, lo:lo + rank].astype(BF16)
        logit = _dot(r, wd_ref[...].astype(BF16)) + bd_ref[...]
        la = (jnp.minimum(logit, 0.0) - jnp.log(1.0 + jnp.exp(-jnp.abs(logit)))) / GLA_TAU
        _scan_block(q_ref[...] * q_scale, k_ref[...], v_ref[...], la,
                    slow_ref, mask_ref, st_ref, u_ref, o_scr, c=c, reverse=reverse)
        if last_pass:
            _scan_finish(o_scr, prev_ref, gate_ref, gn_ref, y_ref)

    _scan_state_io(s0_ref, sT_ref, st_ref, body)


def _scan_call(kernel, proj, col_specs, extra, s0, prev, gate_col, gn, nb, t, heads, dk, dv, row0,
               reverse, out_row0=0, out_rows=None, carry=None, **kw):
    tb = _pick_tile(t, (SCAN_BLOCK, 128))
    c = min(SCAN_CHUNK, tb)
    nblk = t // tb
    rb0 = row0 // tb
    last_pass = prev is not None
    slow, masks = _scan_constants(c, reverse)

    def tblk(i):
        return (nblk - 1 - i) if reverse else i

    hb = _pick_tile(heads, (SCAN_HEADS, 2, 1))

    def col(off, width, per_head=True):
        if not per_head:
            return pl.BlockSpec((tb, width), lambda b, h, i: (rb0 + b * nblk + tblk(i), off // width))
        assert off % (hb * width) == 0
        return pl.BlockSpec((tb, hb * width),
                            lambda b, h, i: (rb0 + b * nblk + tblk(i), off // (hb * width) + h))

    in_specs = [col(*spec[1:]) for spec in col_specs]
    args = [spec[0] for spec in col_specs]
    for e in extra:
        in_specs.append(pl.BlockSpec((e.shape[0], hb * dk), lambda b, h, i: (0, h)))
        args.append(e)
    in_specs.append(pl.BlockSpec((None, hb, dk, dv), lambda b, h, i: (b, h, 0, 0)))
    args.append(s0)
    for cst in (slow, masks):
        in_specs.append(pl.BlockSpec(cst.shape, lambda b, h, i, nd=cst.ndim: (0,) * nd))
        args.append(cst)
    scratch = [pltpu.VMEM((hb, dv, dk), F32), pltpu.VMEM((hb * (tb // c), dv, dk), F32)]
    if last_pass:
        in_specs += [pl.BlockSpec((tb, hb * dv), lambda b, h, i: (b * nblk + tblk(i), h)),
                     col(gate_col, dv),
                     pl.BlockSpec((1, dv), lambda b, h, i: (0, 0))]
        args += [prev, proj, gn]
        scratch.append(pltpu.VMEM((tb, hb * dv), F32))
    y_dtype = BF16 if last_pass else F32
    orb = out_row0 // tb
    out_rows = nb * t if out_rows is None else out_rows
    aliases, carry_pos = {}, None
    if carry is not None:
        carry_pos = len(args)
        aliases = {carry_pos: 0}
        in_specs.append(pl.BlockSpec(memory_space=pl.ANY))
        args.append(carry)
    return pl.pallas_call(
        functools.partial(kernel, c=c, reverse=reverse, last_pass=last_pass, carry_pos=carry_pos, **kw),
        grid=(nb, heads // hb, nblk),
        in_specs=in_specs,
        out_specs=[pl.BlockSpec((tb, hb * dv), lambda b, h, i: (orb + b * nblk + tblk(i), h)),
                   pl.BlockSpec((None, hb, dk, dv), lambda b, h, i: (b, h, 0, 0))],
        out_shape=[jax.ShapeDtypeStruct((out_rows, heads * dv), y_dtype),
                   jax.ShapeDtypeStruct((nb, heads, dk, dv), F32)],
        input_output_aliases=aliases,
        scratch_shapes=scratch,
        compiler_params=_params(("parallel", "parallel", "arbitrary")),
    )(*args)


class _Geometry:
    pass


def _rope_tables(n_tok, da):
    ax = da // 2
    rows = n_tok // GRID_W
    r = jnp.broadcast_to(jnp.arange(rows, dtype=F32)[:, None], (rows, GRID_W)).reshape(-1)
    col = jnp.broadcast_to(jnp.arange(GRID_W, dtype=F32)[None, :], (rows, GRID_W)).reshape(-1)
    inv = ROPE_BASE ** (-jnp.arange(0, ax, 2, dtype=F32) / ax)
    ar, ac = r[:, None] * inv, col[:, None] * inv
    ang = jnp.concatenate([ar, ar, ac, ac], axis=-1)
    cos, sin = jnp.cos(ang), jnp.sin(ang)
    q = da // 4
    sign = jnp.where((jnp.arange(da) % (2 * q)) < q, -1.0, 1.0).astype(F32)
    return jnp.tile(cos, (1, 2)), jnp.tile(sin * sign, (1, 2))


def kernel(x_prompt, x_sample, c, cache_attn_k, cache_attn_v, state_hgrn, state_gla, c_ctx, ada_w, ada_b, norm_w, ffn_w_gate, ffn_w_up, ffn_w_down, w_in, attn_qk_norm, attn_lambda, attn_out_norm, hgrn_lb_logits, hgrn_out_norm, gla_w_decay, gla_b_decay, gla_out_norm, w_branch, w_out):
    batch, seq, d = x_prompt.shape
    dec_batch, dec_seq, _ = x_sample.shape
    depth = ada_w.shape[0]
    ha, da = cache_attn_k.shape[2], cache_attn_k.shape[5]
    dva = 2 * da
    hh, hdk, hdv = state_hgrn.shape[3:]
    hg, gdk, gdv = state_gla.shape[3:]
    rank = gla_w_decay.shape[2]
    sizes = (ha * 2 * da, ha * 2 * da, ha * dva, hh * hdk, hh * hdk, hh * hdk, hh * hdv, hh * hdv,
             hg * gdk, hg * gdk, hg * gdv, hg * gdv, 2 * rank, 3 * d)
    assert sum(sizes) == w_in.shape[2] and hdk == LANES and hdv == LANES and gdk == LANES and dva == LANES

    geo = _Geometry()
    geo.n_ctx_rows = batch * seq
    geo.dec_seq = dec_seq
    geo.tile_rows = math.gcd(geo.n_ctx_rows, dec_seq)
    n_ctx = geo.n_ctx_rows

    src = [0] + [int(v) for v in np.cumsum(sizes)]
    n_main = src[12]
    names = ('aq', 'ak', 'av', 'hq', 'hf_f', 'hf_b', 'hi', 'hg', 'gq', 'gk', 'gv', 'gg')
    off = {nm: src[i] for i, nm in enumerate(names)}

    n_cond = 1 + dec_batch
    cond = jnp.concatenate([c_ctx[None], c, jnp.zeros((-n_cond % 8, d), F32)], axis=0)
    mod = _adaln(cond, ada_w, ada_b).reshape(depth, cond.shape[0], 9, d)

    lb_w = jax.nn.softmax(hgrn_lb_logits.astype(F32), axis=0)
    lower_bounds = jnp.cumsum(lb_w, axis=0) - lb_w[0]
    rope_tabs = _rope_tables(dec_seq, da)

    n_lat = dec_batch * dec_seq
    nt = n_ctx + n_lat
    zeros_h = jnp.zeros((batch, hh, hdk, hdv), F32)
    zeros_g = jnp.zeros((batch, hg, gdk, gdv), F32)
    ks_l, vs_l, sh_l, sg_l = [], [], [], []
    x = None
    wg_bf, wu_bf, wd_bf = ffn_w_gate.astype(BF16), ffn_w_up.astype(BF16), ffn_w_down.astype(BF16)
    wb_bf, wo_bf = w_branch.astype(BF16), w_out.astype(BF16)
    w_main_bf = w_in[:, :, :n_main].astype(BF16)
    w_glr_bf = jnp.pad(w_in[:, :, src[12]:src[13]], ((0, 0), (0, 0), (0, LANES - 2 * rank))).astype(BF16)
    w_gate_bf = w_in[:, :, src[13]:].astype(BF16)
    cache_k = cache_attn_k.reshape(cache_attn_k.shape[:4] + (2 * da,))

    for l in range(depth):
        m_l = mod[l]
        nw = norm_w[l]
        ffn1 = (m_l, nw[0:1], wg_bf, wu_bf, wd_bf, (l, 0), 0, geo)
        if l == 0:
            part = _ffn(x_prompt.reshape(n_ctx, d), *ffn1, out_rows=nt, nxt=(nw[1:2], 3))
            x, h_mix = _ffn(x_sample.reshape(n_lat, d), *ffn1, token_row0=n_ctx, out_row0=n_ctx,
                            out_rows=nt, carry=part, nxt=(nw[1:2], 3))
        else:
            x, h_mix = _ffn(x, *ffn1, nxt=(nw[1:2], 3))
        proj = _matmul(h_mix, w_main_bf, F32, layer=l)
        glr = _matmul(h_mix, w_glr_bf, F32, layer=l)
        gates = _matmul(h_mix, w_gate_bf, BF16, gate=True, layer=l)

        lam_init = 0.8 - 0.6 * math.exp(-0.3 * l)
        gains = jnp.tile(attn_qk_norm[l], (1, 2))
        cols = (off['aq'], off['ak'], off['av'])
        gn_a = attn_out_norm[l][:, None]
        qp, kp, vp, k32, v32 = _attn_prep(proj, cols, gains, None, batch, seq, ha, da, 0, True)
        ks_l.append(k32.reshape(batch, ha, seq, 2, da))
        vs_l.append(v32)
        qs, ksm, vsm = _attn_prep(proj, cols, gains, rope_tabs, dec_batch, dec_seq, ha, da, n_ctx, False)
        attn = (attn_lambda[l], gn_a, lam_init, da)
        br_a = _attention(qp, kp, vp, *attn, out_rows=nt)
        br_a = _attention(qs, ksm, vsm, *attn, ctx=(cache_k, cache_attn_v, l), out_row0=n_ctx,
                          out_rows=nt, carry=br_a)

        lb = lower_bounds[l][:, None]
        gn_h = hgrn_out_norm[l][None]

        def hgrn(nb, t, row0, s0, carry):
            specs_b = ((proj, off['hq'], hdk), (proj, off['hf_b'], hdk), (proj, off['hi'], hdv))
            specs_f = ((proj, off['hq'], hdk), (proj, off['hf_f'], hdk), (proj, off['hi'], hdv))
            o_b, s_b = _scan_call(_hgrn_kernel, proj, specs_b, [lb[1]], s0[1], None, None, None,
                                  nb, t, hh, hdk, hdv, row0, True)
            y, s_f = _scan_call(_hgrn_kernel, proj, specs_f, [lb[0]], s0[0], o_b, off['hg'], gn_h,
                                nb, t, hh, hdk, hdv, row0, False, out_row0=row0, out_rows=nt, carry=carry)
            return y, jnp.stack([s_f, s_b], axis=1)

        br_h, s_h = hgrn(batch, seq, 0, (zeros_h, zeros_h), None)
        br_h, _ = hgrn(dec_batch, dec_seq, n_ctx, (state_hgrn[:, l, 0], state_hgrn[:, l, 1]), br_h)
        sh_l.append(s_h)

        wd = gla_w_decay[l]
        bd = gla_b_decay[l][:, None]
        gn_g = gla_out_norm[l][None]

        def gla(nb, t, row0, s0, carry):
            specs = ((proj, off['gq'], gdk), (proj, off['gk'], gdk), (proj, off['gv'], gdv),
                     (glr, 0, LANES, False))
            kw = dict(rank=rank, q_scale=gdk ** -0.5)
            o_b, s_b = _scan_call(_gla_kernel, proj, specs, [wd[1], bd[1]], s0[1], None, None, None,
                                  nb, t, hg, gdk, gdv, row0, True, **kw)
            y, s_f = _scan_call(_gla_kernel, proj, specs, [wd[0], bd[0]], s0[0], o_b, off['gg'], gn_g,
                                nb, t, hg, gdk, gdv, row0, False, out_row0=row0, out_rows=nt, carry=carry,
                                **kw)
            return y, jnp.stack([s_f, s_b], axis=1)

        br_g, s_g = gla(batch, seq, 0, (zeros_g, zeros_g), None)
        br_g, _ = gla(dec_batch, dec_seq, n_ctx, (state_gla[:, l, 0], state_gla[:, l, 1]), br_g)
        sg_l.append(s_g)

        x = _mix_out(br_a, br_h, br_g, gates, wb_bf, wo_bf, l, x, m_l, 5, geo)
        ffn2 = (m_l, nw[2:3], wg_bf, wu_bf, wd_bf, (l, 1), 6, geo)
        if l + 1 < depth:
            x = _ffn(x, *ffn2)[0]
        else:
            y_prompt = _ffn(x, *ffn2, rows=n_ctx)[0].reshape(batch, seq, d)
            y_sample = _ffn(x, *ffn2, rows=n_lat, in_row0=n_ctx,
                            token_row0=n_ctx)[0].reshape(dec_batch, dec_seq, d)

    return (y_prompt, y_sample, jnp.stack(ks_l, axis=1), jnp.stack(vs_l, axis=1),
            jnp.stack(sh_l, axis=1), jnp.stack(sg_l, axis=1))
```

```python
import functools
import math

import jax
import jax.numpy as jnp
import numpy as np
from jax import lax
from jax.experimental import pallas as pl
from jax.experimental.pallas import tpu as pltpu

F32 = jnp.float32
BF16 = jnp.bfloat16

GRID_W = 64
GLA_TAU = 16.0
ROPE_BASE = 10000.0
EPS = 1e-6
LANES = 128
SCAN_CHUNK = 64
SCAN_BLOCK = 256
SCAN_HEADS = 4
ATTN_LOOKAHEAD = 2
SAFE_EXPONENT = 80.0
VMEM_LIMIT = 56 * 1024 * 1024


def _dot(a, b):
    return jnp.dot(a, b, preferred_element_type=F32)


def _dot_nt(a, b):
    return lax.dot_general(a, b, (((1,), (1,)), ((), ())), preferred_element_type=F32)


def _dot_tn(a, b):
    return lax.dot_general(a, b, (((0,), (0,)), ((), ())), preferred_element_type=F32)


def _sigmoid(x):
    return 1.0 / (1.0 + jnp.exp(-x))


def _silu(x):
    return x * _sigmoid(x)


def _split3(x):
    hi = x.astype(BF16)
    r1 = x - hi.astype(F32)
    mid = r1.astype(BF16)
    lo = (r1 - mid.astype(F32)).astype(BF16)
    return hi, mid, lo


def _dot_exact_lhs(m, parts):
    return _dot(m, parts[0]) + _dot(m, parts[1]) + _dot(m, parts[2])


def _pick_tile(n, candidates):
    for c in candidates:
        if n % c == 0:
            return c
    return n


def _params(sem):
    return pltpu.CompilerParams(dimension_semantics=sem, vmem_limit_bytes=VMEM_LIMIT)


def _adaln_kernel(c_ref, w_ref, b_ref, o_ref):
    s = _silu(c_ref[...]).astype(BF16)
    o_ref[...] = _dot(s, w_ref[...].astype(BF16)) + b_ref[...]


def _adaln(cond, ada_w, ada_b):
    depth, d, n = ada_w.shape
    rows = cond.shape[0]
    tn = _pick_tile(n, (1024, 512, 256, 128))
    return pl.pallas_call(
        _adaln_kernel,
        grid=(depth, n // tn),
        in_specs=[pl.BlockSpec((rows, d), lambda l, j: (0, 0)),
                  pl.BlockSpec((None, d, tn), lambda l, j: (l, 0, j)),
                  pl.BlockSpec((None, 1, tn), lambda l, j: (l, 0, j))],
        out_specs=pl.BlockSpec((None, rows, tn), lambda l, j: (l, 0, j)),
        out_shape=jax.ShapeDtypeStruct((depth, rows, n), F32),
        compiler_params=_params(("parallel", "parallel")),
    )(cond, ada_w, ada_b.reshape(depth, 1, n))


def _modulated(x, nw, shift, scale):
    y = x * lax.rsqrt(jnp.mean(x * x, axis=-1, keepdims=True) + EPS) * nw
    return y * (1.0 + scale) + shift


def _ffn_kernel(*refs, base, next_base):
    x_ref, m_ref, nw_ref, wg_ref, wu_ref, wd_ref = refs[:6]
    if next_base is not None:
        nw_next_ref, o_ref, h_next_ref, h_ref = refs[6:]
    else:
        o_ref, h_ref = refs[6:]
    j = pl.program_id(1)

    @pl.when(j == 0)
    def _():
        h = _modulated(x_ref[...], nw_ref[...], m_ref[base:base + 1, :], m_ref[base + 1:base + 2, :])
        h_ref[...] = h.astype(BF16)
        o_ref[...] = jnp.zeros_like(o_ref)

    h = h_ref[...]
    g = _dot(h, wg_ref[...])
    u = _dot(h, wu_ref[...])
    o_ref[...] += _dot((_silu(g) * u).astype(BF16), wd_ref[...])

    @pl.when(j == pl.num_programs(1) - 1)
    def _():
        y = x_ref[...] + 0.5 * m_ref[base + 2:base + 3, :] * o_ref[...]
        o_ref[...] = y
        if next_base is not None:
            h_next_ref[...] = _modulated(y, nw_next_ref[...], m_ref[next_base:next_base + 1, :],
                                         m_ref[next_base + 1:next_base + 2, :]).astype(BF16)


def _cond_index(i, tm, row0, n_ctx_rows, seq_rows):
    r = row0 + i * tm
    return jnp.where(r < n_ctx_rows, 0, (r - n_ctx_rows) // seq_rows + 1)


def _ffn(x, mod, nw, wg, wu, wd, wsel, base, geo, *, rows=None, in_row0=0, nxt=None):
    d = x.shape[1]
    rows = x.shape[0] if rows is None else rows
    f = wg.shape[-1]
    tm = _pick_tile(geo.tile_rows, (512, 256, 128))
    tf = _pick_tile(f, (512, 256, 128))
    cidx = functools.partial(_cond_index, tm=tm, row0=in_row0, n_ctx_rows=geo.n_ctx_rows,
                             seq_rows=geo.dec_seq)
    ib = in_row0 // tm
    l, s = wsel
    in_specs = [pl.BlockSpec((tm, d), lambda i, j: (i + ib, 0)),
                pl.BlockSpec((None, 9, d), lambda i, j: (cidx(i), 0, 0)),
                pl.BlockSpec((1, d), lambda i, j: (0, 0)),
                pl.BlockSpec((None, None, d, tf), lambda i, j: (l, s, 0, j)),
                pl.BlockSpec((None, None, d, tf), lambda i, j: (l, s, 0, j)),
                pl.BlockSpec((None, None, tf, d), lambda i, j: (l, s, j, 0))]
    args = [x, mod, nw, wg, wu, wd]
    out_specs = [pl.BlockSpec((tm, d), lambda i, j: (i, 0))]
    out_shape = [jax.ShapeDtypeStruct((rows, d), F32)]
    if nxt is not None:
        in_specs.append(pl.BlockSpec((1, d), lambda i, j: (0, 0)))
        args.append(nxt[0])
        out_specs.append(pl.BlockSpec((tm, d), lambda i, j: (i, 0)))
        out_shape.append(jax.ShapeDtypeStruct((rows, d), BF16))
    return pl.pallas_call(
        functools.partial(_ffn_kernel, base=base, next_base=None if nxt is None else nxt[1]),
        grid=(rows // tm, f // tf),
        in_specs=in_specs,
        out_specs=out_specs,
        out_shape=out_shape,
        scratch_shapes=[pltpu.VMEM((tm, d), BF16)],
        compiler_params=_params(("parallel", "arbitrary")),
    )(*args)


def _matmul_kernel(a_ref, w_ref, o_ref, *, gate):
    y = _dot(a_ref[...], w_ref[...].astype(BF16))
    o_ref[...] = (_sigmoid(y) if gate else y).astype(o_ref.dtype)


def _matmul(a, w, out_dtype, gate=False, layer=None, n=None):
    nt, k = a.shape
    n = w.shape[-1] if n is None else n
    tm = _pick_tile(nt, (1024, 512, 256, 128))
    tn = _pick_tile(n, (1024, 768, 512, 256, 128))
    if layer is None:
        w_spec = pl.BlockSpec((k, tn), lambda i, j: (0, j))
    else:
        w_spec = pl.BlockSpec((None, k, tn), lambda i, j: (layer, 0, j))
    return pl.pallas_call(
        functools.partial(_matmul_kernel, gate=gate),
        grid=(nt // tm, n // tn),
        in_specs=[pl.BlockSpec((tm, k), lambda i, j: (i, 0)), w_spec],
        out_specs=pl.BlockSpec((tm, tn), lambda i, j: (i, j)),
        out_shape=jax.ShapeDtypeStruct((nt, n), out_dtype),
        compiler_params=_params(("parallel", "parallel")),
    )(a, w)


def _mix_out_kernel(ba_ref, bh_ref, bg_ref, g_ref, wb_ref, wo_ref, x_ref, m_ref, o_ref, *, row):
    d = x_ref.shape[1]
    acc = g_ref[:, 0:d].astype(F32) * _dot(ba_ref[...], wb_ref[0])
    acc += g_ref[:, d:2 * d].astype(F32) * _dot(bh_ref[...], wb_ref[1])
    acc += g_ref[:, 2 * d:3 * d].astype(F32) * _dot(bg_ref[...], wb_ref[2])
    o_ref[...] = x_ref[...] + m_ref[row:row + 1, :] * _dot(acc.astype(BF16), wo_ref[...])


def _mix_out(br_a, br_h, br_g, gates, wb, wo, layer, x, mod, row, geo):
    nt, bw = br_a.shape
    d = x.shape[1]
    tm = _pick_tile(geo.tile_rows, (256, 128))
    cidx = functools.partial(_cond_index, tm=tm, row0=0, n_ctx_rows=geo.n_ctx_rows, seq_rows=geo.dec_seq)
    bspec = pl.BlockSpec((tm, bw), lambda i: (i, 0))
    return pl.pallas_call(
        functools.partial(_mix_out_kernel, row=row),
        grid=(nt // tm,),
        in_specs=[bspec, bspec, bspec,
                  pl.BlockSpec((tm, 3 * d), lambda i: (i, 0)),
                  pl.BlockSpec((None, 3, bw, d), lambda i: (layer, 0, 0, 0), pipeline_mode=pl.Buffered(1)),
                  pl.BlockSpec((None, d, d), lambda i: (layer, 0, 0), pipeline_mode=pl.Buffered(1)),
                  pl.BlockSpec((tm, d), lambda i: (i, 0)),
                  pl.BlockSpec((None, 9, d), lambda i: (cidx(i), 0, 0))],
        out_specs=pl.BlockSpec((tm, d), lambda i: (i, 0)),
        out_shape=jax.ShapeDtypeStruct((nt, d), F32),
        compiler_params=_params(("parallel",)),
    )(br_a, br_h, br_g, gates, wb, wo, x, mod)


def _half_mean_matrix(da):
    m = np.kron(np.eye(2), np.ones((da, da))) / da
    return jnp.asarray(m, BF16)


def _qk_normed(x, gmat, gain):
    sq = x * x
    hi = sq.astype(BF16)
    lo = (sq - hi.astype(F32)).astype(BF16)
    msq = _dot(hi, gmat) + _dot(lo, gmat)
    return x * lax.rsqrt(msq + EPS) * gain


def _rotated(x, cos, sin_signed, da):
    q = da // 4
    width = x.shape[-1]
    lane = lax.broadcasted_iota(jnp.int32, x.shape, 1)
    first = (lane % (2 * q)) < q
    partner = jnp.where(first, pltpu.roll(x, width - q, axis=1), pltpu.roll(x, q, axis=1))
    return x * cos + partner * sin_signed


def _attn_prep_kernel(*refs, rope, emit_f32, da):
    if rope:
        q_ref, k_ref, v_ref, gm_ref, gn_ref, cos_ref, sin_ref = refs[:7]
        outs = refs[7:]
    else:
        q_ref, k_ref, v_ref, gm_ref, gn_ref = refs[:5]
        outs = refs[5:]
    gm = gm_ref[...]
    w = 2 * da
    for h in range(q_ref.shape[1] // w):
        hc = slice(h * w, (h + 1) * w)
        q = _qk_normed(q_ref[:, hc], gm, gn_ref[0:1, :])
        k = _qk_normed(k_ref[:, hc], gm, gn_ref[1:2, :])
        if rope:
            q = _rotated(q, cos_ref[...], sin_ref[...], da)
            k = _rotated(k, cos_ref[...], sin_ref[...], da)
        v = v_ref[:, hc]
        outs[0][h] = (q * (math.log2(math.e) * da ** -0.5)).T.astype(BF16)
        outs[1][h] = k.astype(BF16)
        outs[2][h] = v.T.astype(BF16)
        if emit_f32:
            outs[3][h] = k
            outs[4][h] = v


def _attn_prep(proj, cols, gains, rope_tabs, nb, t, heads, da, row0, emit_f32):
    w = 2 * da
    tt = _pick_tile(t, (256, 128))
    nblk = t // tt
    rb0 = row0 // tt
    hw = heads * w
    assert all(c % hw == 0 for c in cols)
    rope = rope_tabs is not None

    def col(c0):
        return pl.BlockSpec((tt, hw), lambda b, i: (rb0 + b * nblk + i, c0 // hw))

    in_specs = [col(cols[0]), col(cols[1]), col(cols[2]),
                pl.BlockSpec((w, w), lambda b, i: (0, 0)),
                pl.BlockSpec((2, w), lambda b, i: (0, 0))]
    args = [proj, proj, proj, _half_mean_matrix(da), gains]
    if rope:
        in_specs += [pl.BlockSpec((tt, w), lambda b, i: (i, 0))] * 2
        args += list(rope_tabs)
    rows = pl.BlockSpec((None, heads, tt, w), lambda b, i: (b, 0, i, 0))
    cols_t = pl.BlockSpec((None, heads, w, tt), lambda b, i: (b, 0, 0, i))
    out_specs = [cols_t, rows, cols_t]
    out_shape = [jax.ShapeDtypeStruct((nb, heads, w, t), BF16),
                 jax.ShapeDtypeStruct((nb, heads, t, w), BF16),
                 jax.ShapeDtypeStruct((nb, heads, w, t), BF16)]
    if emit_f32:
        out_specs += [rows, rows]
        out_shape += [jax.ShapeDtypeStruct((nb, heads, t, w), F32)] * 2
    return pl.pallas_call(
        functools.partial(_attn_prep_kernel, rope=rope, emit_f32=emit_f32, da=da),
        grid=(nb, nblk),
        in_specs=in_specs,
        out_specs=out_specs,
        out_shape=out_shape,
        compiler_params=_params(("parallel", "parallel")),
    )(*args)


def _attn_kernel(*refs, tk, lam_init, da, has_ctx):
    if has_ctx:
        qt_ref, k_ref, vt_ref, ck_ref, cv_ref, lam_ref, g_ref = refs[:7]
        n_ctx = ck_ref.shape[0] // tk
    else:
        qt_ref, k_ref, vt_ref, lam_ref, g_ref = refs[:5]
        n_ctx = 0
    o_ref = refs[-1]
    nk = n_ctx + k_ref.shape[0] // tk
    qt = qt_ref[...]
    feat = lax.broadcasted_iota(jnp.int32, qt.shape, 0)
    zero = jnp.zeros_like(qt)

    tq = qt.shape[1]
    q2t = jnp.concatenate([jnp.where(feat < da, qt, zero), jnp.where(feat >= da, qt, zero)], axis=1)

    def scores(i):
        if i < n_ctx:
            k_tile = ck_ref[i * tk:(i + 1) * tk, :].astype(BF16)
        else:
            k_tile = k_ref[(i - n_ctx) * tk:(i - n_ctx + 1) * tk, :]
        return _dot(k_tile, q2t)

    def values_t(i):
        if i < n_ctx:
            return cv_ref[i * tk:(i + 1) * tk, :].T.astype(BF16)
        return vt_ref[:, (i - n_ctx) * tk:(i - n_ctx + 1) * tk]

    ahead = [scores(i) for i in range(min(ATTN_LOOKAHEAD, nk))]
    m = l = acc = None
    for i in range(nk):
        s = ahead.pop(0)
        if i + ATTN_LOOKAHEAD < nk:
            ahead.append(scores(i + ATTN_LOOKAHEAD))
        m_tile = jnp.max(s, axis=0, keepdims=True)
        m_new = m_tile if i == 0 else jnp.maximum(m, m_tile)
        p = jnp.exp2(s - m_new)
        pv = _dot(values_t(i), p.astype(BF16))
        if i == 0:
            l, acc = jnp.sum(p, axis=0, keepdims=True), pv
        else:
            alpha = jnp.exp2(m - m_new)
            l = alpha * l + jnp.sum(p, axis=0, keepdims=True)
            acc = alpha * acc + pv
        m = m_new
    o = acc / l
    lam_p = lam_ref[...]
    lam = (jnp.exp(jnp.sum(lam_p[0:1, :] * lam_p[1:2, :], axis=-1, keepdims=True))
           - jnp.exp(jnp.sum(lam_p[2:3, :] * lam_p[3:4, :], axis=-1, keepdims=True)) + lam_init)
    o = o[:, :tq] - lam * o[:, tq:]
    o = o * lax.rsqrt(jnp.mean(o * o, axis=0, keepdims=True) + EPS) * g_ref[...]
    o_ref[...] = (o * (1.0 - lam_init)).T.astype(BF16)


def _attention(qt, k, vt, lam_p, gain, lam_init, da, *, ctx=None, out_row0=0, out_rows=None, carry=None):
    nb, heads, w, n = qt.shape
    m = k.shape[2]
    tq = _pick_tile(n, (256, 128))
    tk = _pick_tile(math.gcd(m, ctx[0].shape[3]) if ctx else m, (256, 128))
    nq = n // tq
    orb = out_row0 // tq
    out_rows = nb * n if out_rows is None else out_rows
    in_specs = [pl.BlockSpec((None, None, w, tq), lambda b, h, i: (b, h, 0, i)),
                pl.BlockSpec((None, None, m, w), lambda b, h, i: (b, h, 0, 0)),
                pl.BlockSpec((None, None, w, m), lambda b, h, i: (b, h, 0, 0))]
    args = [qt, k, vt]
    if ctx:
        ck, cv, layer = ctx
        spec = pl.BlockSpec((None, None, None) + ck.shape[3:], lambda b, h, i: (b, layer, h, 0, 0))
        in_specs += [spec, spec]
        args += [ck, cv]
    in_specs += [pl.BlockSpec(lam_p.shape, lambda b, h, i: (0, 0)),
                 pl.BlockSpec((w, 1), lambda b, h, i: (0, 0))]
    args += [lam_p, gain]
    aliases = {}
    if carry is not None:
        aliases = {len(args): 0}
        in_specs.append(pl.BlockSpec(memory_space=pl.ANY))
        args.append(carry)
    return pl.pallas_call(
        functools.partial(_attn_kernel, tk=tk, lam_init=lam_init, da=da, has_ctx=bool(ctx)),
        grid=(nb, heads, nq),
        in_specs=in_specs,
        out_specs=pl.BlockSpec((tq, w), lambda b, h, i: (orb + b * nq + i, h)),
        out_shape=jax.ShapeDtypeStruct((out_rows, heads * w), BF16),
        input_output_aliases=aliases,
        compiler_params=_params(("parallel", "parallel", "parallel")),
    )(*args)


def _scan_constants(c, reverse):
    i = np.arange(c)[:, None]
    t = np.arange(c)[None, :]

    levels, masks = [], []
    g = c
    while g >= 2:
        h = g // 2
        mat = np.zeros((c, c), np.float32)
        for r in range(c):
            mid = (r // g) * g + h
            if not reverse:
                if r >= mid:
                    mat[r, mid:r + 1] = 1.0
                else:
                    mat[r, r + 1:mid] = 1.0
            else:
                if r < mid:
                    mat[r, r:mid] = 1.0
                else:
                    mat[r, mid:r] = 1.0
        same = (i // g) == (t // g)
        if not reverse:
            ok = same & ((i % g) >= h) & ((t % g) < h)
        else:
            ok = same & ((i % g) < h) & ((t % g) >= h)
        levels.append(mat)
        masks.append(ok.astype(np.float32))
        g = h
    masks.append(np.eye(c, dtype=np.float32))
    return jnp.asarray(np.concatenate(levels, axis=0), BF16), jnp.asarray(np.stack(masks), F32)


def _chunk_cumsum(x, c, reverse):
    tb = x.shape[0]
    pos = lax.broadcasted_iota(jnp.int32, x.shape, 0) % c
    s = 1
    while s < c:
        if reverse:
            x = x + jnp.where(pos < c - s, pltpu.roll(x, tb - s, axis=0), 0.0)
        else:
            x = x + jnp.where(pos >= s, pltpu.roll(x, s, axis=0), 0.0)
        s *= 2
    return x


def _scan_block(q, k, v, la, slow_ref, mask_ref, st_ref, u_ref, o_ref, *, c, reverse):
    tb = q.shape[0]
    n = tb // c
    hb, dv, dk = st_ref.shape
    chunks = [slice(ci * c, (ci + 1) * c) for ci in range(n)]
    kcols = [slice(j * dk, (j + 1) * dk) for j in range(hb)]
    vcols = [slice(j * dv, (j + 1) * dv) for j in range(hb)]
    ref_row = c // 2 if reverse else c // 2 - 1
    end_row = 0 if reverse else c - 1
    b = _chunk_cumsum(la, c, reverse)
    b_ref = jnp.concatenate([jnp.broadcast_to(b[sl][ref_row:ref_row + 1], (c, b.shape[1])) for sl in chunks], axis=0)
    b_end = jnp.concatenate([jnp.broadcast_to(b[sl][end_row:end_row + 1], (c, b.shape[1])) for sl in chunks], axis=0)
    zq = b - b_ref
    safe = jnp.max(jnp.abs(zq)) < SAFE_EXPONENT
    q_in = (q * jnp.exp(b)).astype(BF16)
    k_up = (k * jnp.exp(b_end - b)).astype(BF16)
    v_bf = v.astype(BF16)
    row = lax.broadcasted_iota(jnp.int32, (c, c), 0)
    col = lax.broadcasted_iota(jnp.int32, (c, c), 1)
    visible = (col >= row) if reverse else (col <= row)
    n_levels = slow_ref.shape[0] // c

    def fast_scores():
        q_mid = (q * jnp.exp(zq)).astype(BF16)
        k_mid = (k * jnp.exp(-zq)).astype(BF16)
        return jnp.concatenate([jnp.where(visible, _dot_nt(q_mid[sl, kc], k_mid[sl, kc]), 0.0)
                                for kc in kcols for sl in chunks], axis=0)

    def slow_scores():
        q_bf = q.astype(BF16)
        k_bf = k.astype(BF16)
        parts = _split3(la)
        out = []
        for kc in kcols:
            for sl in chunks:
                zs = _dot_exact_lhs(slow_ref[...], tuple(p[sl, kc] for p in parts))
                a = mask_ref[n_levels] * _dot_nt(q_bf[sl, kc], k_bf[sl, kc])
                for lv in range(n_levels):
                    e = jnp.exp(zs[lv * c:(lv + 1) * c])
                    a += mask_ref[lv] * _dot_nt((q[sl, kc] * e).astype(BF16), (k[sl, kc] * e).astype(BF16))
                out.append(a)
        return jnp.concatenate(out, axis=0)

    a = lax.cond(safe, fast_scores, slow_scores).astype(BF16)
    for j in range(hb):
        for ci, sl in enumerate(chunks):
            u_ref[j * n + ci] = _dot_tn(v_bf[sl, vcols[j]], k_up[sl, kcols[j]])
    entry = {}
    for j in range(hb):
        st = st_ref[j]
        for ci in (range(n - 1, -1, -1) if reverse else range(n)):
            entry[j, ci] = st.astype(BF16)
            st = st * jnp.exp(b_end[chunks[ci], kcols[j]][0:1]) + u_ref[j * n + ci]
        st_ref[j] = st
    for j in range(hb):
        for ci, sl in enumerate(chunks):
            rows = slice(j * tb + ci * c, j * tb + (ci + 1) * c)
            o_ref[sl, vcols[j]] = _dot_nt(q_in[sl, kcols[j]], entry[j, ci]) + _dot(a[rows], v_bf[sl, vcols[j]])


def _scan_finish(o_ref, prev_ref, gate_ref, gn_ref, y_ref):
    dv = gn_ref.shape[1]
    o = o_ref[...] + prev_ref[...]
    gate = _silu(gate_ref[...])
    for j in range(o.shape[1] // dv):
        cols = slice(j * dv, (j + 1) * dv)
        oj = o[:, cols]
        oj = oj * lax.rsqrt(jnp.mean(oj * oj, axis=-1, keepdims=True) + EPS) * gn_ref[...]
        y_ref[:, cols] = (oj * gate[:, cols]).astype(BF16)


def _scan_state_io(s0_ref, sT_ref, st_ref, do_body):
    t = pl.program_id(2)
    hb = st_ref.shape[0]

    @pl.when(t == 0)
    def _():
        for j in range(hb):
            st_ref[j] = s0_ref[j].T

    do_body()

    @pl.when(t == pl.num_programs(2) - 1)
    def _():
        for j in range(hb):
            sT_ref[j] = st_ref[j].T


def _without(refs, pos):
    return refs if pos is None else refs[:pos] + refs[pos + 1:]


def _hgrn_kernel(*refs, c, reverse, last_pass, carry_pos):
    refs = _without(refs, carry_pos)
    if last_pass:
        (q_ref, f_ref, i_ref, lb_ref, s0_ref, slow_ref, mask_ref, prev_ref, gate_ref, gn_ref,
         y_ref, sT_ref, st_ref, u_ref, o_scr) = refs
    else:
        (q_ref, f_ref, i_ref, lb_ref, s0_ref, slow_ref, mask_ref,
         y_ref, sT_ref, st_ref, u_ref) = refs
        o_scr = y_ref

    def body():
        lb = lb_ref[...]
        f = lb + (1.0 - lb) * _sigmoid(f_ref[...])
        _scan_block(_silu(q_ref[...]), 1.0 - f, i_ref[...], jnp.log(f),
                    slow_ref, mask_ref, st_ref, u_ref, o_scr, c=c, reverse=reverse)
        if last_pass:
            _scan_finish(o_scr, prev_ref, gate_ref, gn_ref, y_ref)

    _scan_state_io(s0_ref, sT_ref, st_ref, body)


def _gla_kernel(*refs, c, reverse, last_pass, carry_pos, rank, q_scale):
    refs = _without(refs, carry_pos)
    if last_pass:
        (q_ref, k_ref, v_ref, r_ref, wd_ref, bd_ref, s0_ref, slow_ref, mask_ref,
         prev_ref, gate_ref, gn_ref, y_ref, sT_ref, st_ref, u_ref, o_scr) = refs
    else:
        (q_ref, k_ref, v_ref, r_ref, wd_ref, bd_ref, s0_ref, slow_ref, mask_ref,
         y_ref, sT_ref, st_ref, u_ref) = refs
        o_scr = y_ref

    def body():
        lo = rank if reverse else 0
        r = r_ref[...][:, lo:lo + rank].astype(BF16)
        logit = _dot(r, wd_ref[...].astype(BF16)) + bd_ref[...]
        la = (jnp.minimum(logit, 0.0) - jnp.log(1.0 + jnp.exp(-jnp.abs(logit)))) / GLA_TAU
        _scan_block(q_ref[...] * q_scale, k_ref[...], v_ref[...], la,
                    slow_ref, mask_ref, st_ref, u_ref, o_scr, c=c, reverse=reverse)
        if last_pass:
            _scan_finish(o_scr, prev_ref, gate_ref, gn_ref, y_ref)

    _scan_state_io(s0_ref, sT_ref, st_ref, body)


def _scan_call(kernel, proj, col_specs, extra, s0, prev, gate_col, gn, nb, t, heads, dk, dv, row0,
               reverse, out_row0=0, out_rows=None, carry=None, **kw):
    tb = _pick_tile(t, (SCAN_BLOCK, 128))
    c = min(SCAN_CHUNK, tb)
    nblk = t // tb
    rb0 = row0 // tb
    last_pass = prev is not None
    slow, masks = _scan_constants(c, reverse)

    def tblk(i):
        return (nblk - 1 - i) if reverse else i

    hb = _pick_tile(heads, (SCAN_HEADS, 2, 1))

    def col(off, width, per_head=True):
        if not per_head:
            return pl.BlockSpec((tb, width), lambda b, h, i: (rb0 + b * nblk + tblk(i), off // width))
        assert off % (hb * width) == 0
        return pl.BlockSpec((tb, hb * width),
                            lambda b, h, i: (rb0 + b * nblk + tblk(i), off // (hb * width) + h))

    in_specs = [col(*spec[1:]) for spec in col_specs]
    args = [spec[0] for spec in col_specs]
    for e in extra:
        in_specs.append(pl.BlockSpec((e.shape[0], hb * dk), lambda b, h, i: (0, h)))
        args.append(e)
    in_specs.append(pl.BlockSpec((None, hb, dk, dv), lambda b, h, i: (b, h, 0, 0)))
    args.append(s0)
    for cst in (slow, masks):
        in_specs.append(pl.BlockSpec(cst.shape, lambda b, h, i, nd=cst.ndim: (0,) * nd))
        args.append(cst)
    scratch = [pltpu.VMEM((hb, dv, dk), F32), pltpu.VMEM((hb * (tb // c), dv, dk), F32)]
    if last_pass:
        in_specs += [pl.BlockSpec((tb, hb * dv), lambda b, h, i: (b * nblk + tblk(i), h)),
                     col(gate_col, dv),
                     pl.BlockSpec((1, dv), lambda b, h, i: (0, 0))]
        args += [prev, proj, gn]
        scratch.append(pltpu.VMEM((tb, hb * dv), F32))
    y_dtype = BF16 if last_pass else F32
    orb = out_row0 // tb
    out_rows = nb * t if out_rows is None else out_rows
    aliases, carry_pos = {}, None
    if carry is not None:
        carry_pos = len(args)
        aliases = {carry_pos: 0}
        in_specs.append(pl.BlockSpec(memory_space=pl.ANY))
        args.append(carry)
    return pl.pallas_call(
        functools.partial(kernel, c=c, reverse=reverse, last_pass=last_pass, carry_pos=carry_pos, **kw),
        grid=(nb, heads // hb, nblk),
        in_specs=in_specs,
        out_specs=[pl.BlockSpec((tb, hb * dv), lambda b, h, i: (orb + b * nblk + tblk(i), h)),
                   pl.BlockSpec((None, hb, dk, dv), lambda b, h, i: (b, h, 0, 0))],
        out_shape=[jax.ShapeDtypeStruct((out_rows, heads * dv), y_dtype),
                   jax.ShapeDtypeStruct((nb, heads, dk, dv), F32)],
        input_output_aliases=aliases,
        scratch_shapes=scratch,
        compiler_params=_params(("parallel", "parallel", "arbitrary")),
    )(*args)


class _Geometry:
    pass


def _rope_tables(n_tok, da):
    ax = da // 2
    rows = n_tok // GRID_W
    r = jnp.broadcast_to(jnp.arange(rows, dtype=F32)[:, None], (rows, GRID_W)).reshape(-1)
    col = jnp.broadcast_to(jnp.arange(GRID_W, dtype=F32)[None, :], (rows, GRID_W)).reshape(-1)
    inv = ROPE_BASE ** (-jnp.arange(0, ax, 2, dtype=F32) / ax)
    ar, ac = r[:, None] * inv, col[:, None] * inv
    ang = jnp.concatenate([ar, ar, ac, ac], axis=-1)
    cos, sin = jnp.cos(ang), jnp.sin(ang)
    q = da // 4
    sign = jnp.where((jnp.arange(da) % (2 * q)) < q, -1.0, 1.0).astype(F32)
    return jnp.tile(cos, (1, 2)), jnp.tile(sin * sign, (1, 2))


def kernel(x_prompt, x_sample, c, cache_attn_k, cache_attn_v, state_hgrn, state_gla, c_ctx, ada_w, ada_b, norm_w, ffn_w_gate, ffn_w_up, ffn_w_down, w_in, attn_qk_norm, attn_lambda, attn_out_norm, hgrn_lb_logits, hgrn_out_norm, gla_w_decay, gla_b_decay, gla_out_norm, w_branch, w_out):
    batch, seq, d = x_prompt.shape
    dec_batch, dec_seq, _ = x_sample.shape
    depth = ada_w.shape[0]
    ha, da = cache_attn_k.shape[2], cache_attn_k.shape[5]
    dva = 2 * da
    hh, hdk, hdv = state_hgrn.shape[3:]
    hg, gdk, gdv = state_gla.shape[3:]
    rank = gla_w_decay.shape[2]
    sizes = (ha * 2 * da, ha * 2 * da, ha * dva, hh * hdk, hh * hdk, hh * hdk, hh * hdv, hh * hdv,
             hg * gdk, hg * gdk, hg * gdv, hg * gdv, 2 * rank, 3 * d)
    assert sum(sizes) == w_in.shape[2] and hdk == LANES and hdv == LANES and gdk == LANES and dva == LANES

    geo = _Geometry()
    geo.n_ctx_rows = batch * seq
    geo.dec_seq = dec_seq
    geo.tile_rows = math.gcd(geo.n_ctx_rows, dec_seq)
    n_ctx = geo.n_ctx_rows

    src = [0] + [int(v) for v in np.cumsum(sizes)]
    n_main = src[12]
    names = ('aq', 'ak', 'av', 'hq', 'hf_f', 'hf_b', 'hi', 'hg', 'gq', 'gk', 'gv', 'gg')
    off = {nm: src[i] for i, nm in enumerate(names)}

    n_cond = 1 + dec_batch
    cond = jnp.concatenate([c_ctx[None], c, jnp.zeros((-n_cond % 8, d), F32)], axis=0)
    mod = _adaln(cond, ada_w, ada_b).reshape(depth, cond.shape[0], 9, d)

    lb_w = jax.nn.softmax(hgrn_lb_logits.astype(F32), axis=0)
    lower_bounds = jnp.cumsum(lb_w, axis=0) - lb_w[0]
    rope_tabs = _rope_tables(dec_seq, da)

    n_lat = dec_batch * dec_seq
    nt = n_ctx + n_lat
    zeros_h = jnp.zeros((batch, hh, hdk, hdv), F32)
    zeros_g = jnp.zeros((batch, hg, gdk, gdv), F32)
    ks_l, vs_l, sh_l, sg_l = [], [], [], []
    x = None
    wg_bf, wu_bf, wd_bf = ffn_w_gate.astype(BF16), ffn_w_up.astype(BF16), ffn_w_down.astype(BF16)
    wb_bf, wo_bf = w_branch.astype(BF16), w_out.astype(BF16)
    w_glr = jnp.pad(w_in[:, :, src[12]:src[13]], ((0, 0), (0, 0), (0, LANES - 2 * rank)))
    w_gate = w_in[:, :, src[13]:]
    cache_k = cache_attn_k.reshape(cache_attn_k.shape[:4] + (2 * da,))
    br_a = br_h = br_g = None

    for l in range(depth):
        m_l = mod[l]
        nw = norm_w[l]
        ffn1 = (m_l, nw[0:1], wg_bf, wu_bf, wd_bf, (l, 0), 0, geo)
        if l == 0:
            x = jnp.concatenate([x_prompt.reshape(n_ctx, d), x_sample.reshape(n_lat, d)], axis=0)
            br_a = jnp.zeros((nt, ha * dva), BF16)
            br_h = jnp.zeros((nt, hh * hdv), BF16)
            br_g = jnp.zeros((nt, hg * gdv), BF16)
        x, h_mix = _ffn(x, *ffn1, nxt=(nw[1:2], 3))
        proj = _matmul(h_mix, w_in, F32, layer=l, n=n_main)
        glr = _matmul(h_mix, w_glr, F32, layer=l)
        gates = _matmul(h_mix, w_gate, BF16, gate=True, layer=l)

        lam_init = 0.8 - 0.6 * math.exp(-0.3 * l)
        gains = jnp.tile(attn_qk_norm[l], (1, 2))
        cols = (off['aq'], off['ak'], off['av'])
        gn_a = attn_out_norm[l][:, None]
        qp, kp, vp, k32, v32 = _attn_prep(proj, cols, gains, None, batch, seq, ha, da, 0, True)
        ks_l.append(k32.reshape(batch, ha, seq, 2, da))
        vs_l.append(v32)
        qs, ksm, vsm = _attn_prep(proj, cols, gains, rope_tabs, dec_batch, dec_seq, ha, da, n_ctx, False)
        attn = (attn_lambda[l], gn_a, lam_init, da)
        br_a = _attention(qp, kp, vp, *attn, out_rows=nt, carry=br_a)
        br_a = _attention(qs, ksm, vsm, *attn, ctx=(cache_k, cache_attn_v, l), out_row0=n_ctx,
                          out_rows=nt, carry=br_a)

        lb = lower_bounds[l][:, None]
        gn_h = hgrn_out_norm[l][None]

        def hgrn(nb, t, row0, s0, carry):
            specs_b = ((proj, off['hq'], hdk), (proj, off['hf_b'], hdk), (proj, off['hi'], hdv))
            specs_f = ((proj, off['hq'], hdk), (proj, off['hf_f'], hdk), (proj, off['hi'], hdv))
            o_b, s_b = _scan_call(_hgrn_kernel, proj, specs_b, [lb[1]], s0[1], None, None, None,
                                  nb, t, hh, hdk, hdv, row0, True)
            y, s_f = _scan_call(_hgrn_kernel, proj, specs_f, [lb[0]], s0[0], o_b, off['hg'], gn_h,
                                nb, t, hh, hdk, hdv, row0, False, out_row0=row0, out_rows=nt, carry=carry)
            return y, jnp.stack([s_f, s_b], axis=1)

        br_h, s_h = hgrn(batch, seq, 0, (zeros_h, zeros_h), br_h)
        br_h, _ = hgrn(dec_batch, dec_seq, n_ctx, (state_hgrn[:, l, 0], state_hgrn[:, l, 1]), br_h)
        sh_l.append(s_h)

        wd = gla_w_decay[l]
        bd = gla_b_decay[l][:, None]
        gn_g = gla_out_norm[l][None]

        def gla(nb, t, row0, s0, carry):
            specs = ((proj, off['gq'], gdk), (proj, off['gk'], gdk), (proj, off['gv'], gdv),
                     (glr, 0, LANES, False))
            kw = dict(rank=rank, q_scale=gdk ** -0.5)
            o_b, s_b = _scan_call(_gla_kernel, proj, specs, [wd[1], bd[1]], s0[1], None, None, None,
                                  nb, t, hg, gdk, gdv, row0, True, **kw)
            y, s_f = _scan_call(_gla_kernel, proj, specs, [wd[0], bd[0]], s0[0], o_b, off['gg'], gn_g,
                                nb, t, hg, gdk, gdv, row0, False, out_row0=row0, out_rows=nt, carry=carry,
                                **kw)
            return y, jnp.stack([s_f, s_b], axis=1)

        br_g, s_g = gla(batch, seq, 0, (zeros_g, zeros_g), br_g)
        br_g, _ = gla(dec_batch, dec_seq, n_ctx, (state_gla[:, l, 0], state_gla[:, l, 1]), br_g)
        sg_l.append(s_g)

        x = _mix_out(br_a, br_h, br_g, gates, wb_bf, wo_bf, l, x, m_l, 5, geo)
        ffn2 = (m_l, nw[2:3], wg_bf, wu_bf, wd_bf, (l, 1), 6, geo)
        if l + 1 < depth:
            x = _ffn(x, *ffn2)[0]
        else:
            y_prompt = _ffn(x, *ffn2, rows=n_ctx)[0].reshape(batch, seq, d)
            y_sample = _ffn(x, *ffn2, rows=n_lat, in_row0=n_ctx)[0].reshape(dec_batch, dec_seq, d)

    return (y_prompt, y_sample, jnp.stack(ks_l, axis=1), jnp.stack(vs_l, axis=1),
            jnp.stack(sh_l, axis=1), jnp.stack(sg_l, axis=1))
```

```python
import functools
import math

import jax
import jax.numpy as jnp
import numpy as np
from jax import lax
from jax.experimental import pallas as pl
from jax.experimental.pallas import tpu as pltpu

F32 = jnp.float32
BF16 = jnp.bfloat16

GRID_W = 64
GLA_TAU = 16.0
ROPE_BASE = 10000.0
EPS = 1e-6
LANES = 128
SCAN_CHUNK = 64
SCAN_BLOCK = 256
SCAN_HEADS = 4
ATTN_LOOKAHEAD = 2
SAFE_EXPONENT = 80.0
VMEM_LIMIT = 56 * 1024 * 1024


def _dot(a, b):
    return jnp.dot(a, b, preferred_element_type=F32)


def _dot_nt(a, b):
    return lax.dot_general(a, b, (((1,), (1,)), ((), ())), preferred_element_type=F32)


def _dot_tn(a, b):
    return lax.dot_general(a, b, (((0,), (0,)), ((), ())), preferred_element_type=F32)


def _sigmoid(x):
    return 1.0 / (1.0 + jnp.exp(-x))


def _silu(x):
    return x * _sigmoid(x)


def _split3(x):
    hi = x.astype(BF16)
    r1 = x - hi.astype(F32)
    mid = r1.astype(BF16)
    lo = (r1 - mid.astype(F32)).astype(BF16)
    return hi, mid, lo


def _dot_exact_lhs(m, parts):
    return _dot(m, parts[0]) + _dot(m, parts[1]) + _dot(m, parts[2])


def _pick_tile(n, candidates):
    for c in candidates:
        if n % c == 0:
            return c
    return n


def _params(sem):
    return pltpu.CompilerParams(dimension_semantics=sem, vmem_limit_bytes=VMEM_LIMIT)


def _adaln_kernel(c_ref, w_ref, b_ref, o_ref):
    s = _silu(c_ref[...]).astype(BF16)
    o_ref[...] = _dot(s, w_ref[...].astype(BF16)) + b_ref[...]


def _adaln(cond, ada_w, ada_b):
    depth, d, n = ada_w.shape
    rows = cond.shape[0]
    tn = _pick_tile(n, (1024, 512, 256, 128))
    return pl.pallas_call(
        _adaln_kernel,
        grid=(depth, n // tn),
        in_specs=[pl.BlockSpec((rows, d), lambda l, j: (0, 0)),
                  pl.BlockSpec((None, d, tn), lambda l, j: (l, 0, j)),
                  pl.BlockSpec((None, 1, tn), lambda l, j: (l, 0, j))],
        out_specs=pl.BlockSpec((None, rows, tn), lambda l, j: (l, 0, j)),
        out_shape=jax.ShapeDtypeStruct((depth, rows, n), F32),
        compiler_params=_params(("parallel", "parallel")),
    )(cond, ada_w, ada_b.reshape(depth, 1, n))


def _modulated(x, nw, shift, scale):
    y = x * lax.rsqrt(jnp.mean(x * x, axis=-1, keepdims=True) + EPS) * nw
    return y * (1.0 + scale) + shift


def _ffn_kernel(*refs, base, next_base):
    x_ref, m_ref, nw_ref, wg_ref, wu_ref, wd_ref = refs[:6]
    if next_base is not None:
        nw_next_ref, o_ref, h_next_ref, h_ref = refs[6:]
    else:
        o_ref, h_ref = refs[6:]
    j = pl.program_id(1)

    @pl.when(j == 0)
    def _():
        h = _modulated(x_ref[...], nw_ref[...], m_ref[base:base + 1, :], m_ref[base + 1:base + 2, :])
        h_ref[...] = h.astype(BF16)
        o_ref[...] = jnp.zeros_like(o_ref)

    h = h_ref[...]
    g = _dot(h, wg_ref[...])
    u = _dot(h, wu_ref[...])
    o_ref[...] += _dot((_silu(g) * u).astype(BF16), wd_ref[...])

    @pl.when(j == pl.num_programs(1) - 1)
    def _():
        y = x_ref[...] + 0.5 * m_ref[base + 2:base + 3, :] * o_ref[...]
        o_ref[...] = y
        if next_base is not None:
            h_next_ref[...] = _modulated(y, nw_next_ref[...], m_ref[next_base:next_base + 1, :],
                                         m_ref[next_base + 1:next_base + 2, :]).astype(BF16)


def _cond_index(i, tm, row0, n_ctx_rows, seq_rows):
    r = row0 + i * tm
    return jnp.where(r < n_ctx_rows, 0, (r - n_ctx_rows) // seq_rows + 1)


def _ffn(x, mod, nw, wg, wu, wd, wsel, base, geo, *, rows=None, in_row0=0, nxt=None):
    d = x.shape[1]
    rows = x.shape[0] if rows is None else rows
    f = wg.shape[-1]
    tm = _pick_tile(geo.tile_rows, (512, 256, 128))
    tf = _pick_tile(f, (512, 256, 128))
    cidx = functools.partial(_cond_index, tm=tm, row0=in_row0, n_ctx_rows=geo.n_ctx_rows,
                             seq_rows=geo.dec_seq)
    ib = in_row0 // tm
    l, s = wsel
    in_specs = [pl.BlockSpec((tm, d), lambda i, j: (i + ib, 0)),
                pl.BlockSpec((None, 9, d), lambda i, j: (cidx(i), 0, 0)),
                pl.BlockSpec((1, d), lambda i, j: (0, 0)),
                pl.BlockSpec((None, None, d, tf), lambda i, j: (l, s, 0, j)),
                pl.BlockSpec((None, None, d, tf), lambda i, j: (l, s, 0, j)),
                pl.BlockSpec((None, None, tf, d), lambda i, j: (l, s, j, 0))]
    args = [x, mod, nw, wg, wu, wd]
    out_specs = [pl.BlockSpec((tm, d), lambda i, j: (i, 0))]
    out_shape = [jax.ShapeDtypeStruct((rows, d), F32)]
    if nxt is not None:
        in_specs.append(pl.BlockSpec((1, d), lambda i, j: (0, 0)))
        args.append(nxt[0])
        out_specs.append(pl.BlockSpec((tm, d), lambda i, j: (i, 0)))
        out_shape.append(jax.ShapeDtypeStruct((rows, d), BF16))
    return pl.pallas_call(
        functools.partial(_ffn_kernel, base=base, next_base=None if nxt is None else nxt[1]),
        grid=(rows // tm, f // tf),
        in_specs=in_specs,
        out_specs=out_specs,
        out_shape=out_shape,
        scratch_shapes=[pltpu.VMEM((tm, d), BF16)],
        compiler_params=_params(("parallel", "arbitrary")),
    )(*args)


def _matmul_kernel(a_ref, w_ref, o_ref, *, gate):
    y = _dot(a_ref[...], w_ref[...])
    o_ref[...] = (_sigmoid(y) if gate else y).astype(o_ref.dtype)


def _matmul(a, w, out_dtype, gate=False):
    nt, k = a.shape
    n = w.shape[1]
    tm = _pick_tile(nt, (1024, 512, 256, 128))
    tn = _pick_tile(n, (1024, 768, 512, 256, 128))
    return pl.pallas_call(
        functools.partial(_matmul_kernel, gate=gate),
        grid=(nt // tm, n // tn),
        in_specs=[pl.BlockSpec((tm, k), lambda i, j: (i, 0)),
                  pl.BlockSpec((k, tn), lambda i, j: (0, j))],
        out_specs=pl.BlockSpec((tm, tn), lambda i, j: (i, j)),
        out_shape=jax.ShapeDtypeStruct((nt, n), out_dtype),
        compiler_params=_params(("parallel", "parallel")),
    )(a, w)


def _mix_out_kernel(ba_ref, bh_ref, bg_ref, g_ref, wb_ref, wo_ref, x_ref, m_ref, o_ref, *, row):
    d = x_ref.shape[1]
    acc = g_ref[:, 0:d].astype(F32) * _dot(ba_ref[...], wb_ref[0])
    acc += g_ref[:, d:2 * d].astype(F32) * _dot(bh_ref[...], wb_ref[1])
    acc += g_ref[:, 2 * d:3 * d].astype(F32) * _dot(bg_ref[...], wb_ref[2])
    o_ref[...] = x_ref[...] + m_ref[row:row + 1, :] * _dot(acc.astype(BF16), wo_ref[...])


def _mix_out(br_a, br_h, br_g, gates, wb, wo, layer, x, mod, row, geo):
    nt, bw = br_a.shape
    d = x.shape[1]
    tm = _pick_tile(geo.tile_rows, (256, 128))
    cidx = functools.partial(_cond_index, tm=tm, row0=0, n_ctx_rows=geo.n_ctx_rows, seq_rows=geo.dec_seq)
    bspec = pl.BlockSpec((tm, bw), lambda i: (i, 0))
    return pl.pallas_call(
        functools.partial(_mix_out_kernel, row=row),
        grid=(nt // tm,),
        in_specs=[bspec, bspec, bspec,
                  pl.BlockSpec((tm, 3 * d), lambda i: (i, 0)),
                  pl.BlockSpec((None, 3, bw, d), lambda i: (layer, 0, 0, 0), pipeline_mode=pl.Buffered(1)),
                  pl.BlockSpec((None, d, d), lambda i: (layer, 0, 0), pipeline_mode=pl.Buffered(1)),
                  pl.BlockSpec((tm, d), lambda i: (i, 0)),
                  pl.BlockSpec((None, 9, d), lambda i: (cidx(i), 0, 0))],
        out_specs=pl.BlockSpec((tm, d), lambda i: (i, 0)),
        out_shape=jax.ShapeDtypeStruct((nt, d), F32),
        compiler_params=_params(("parallel",)),
    )(br_a, br_h, br_g, gates, wb, wo, x, mod)


def _half_mean_matrix(da):
    m = np.kron(np.eye(2), np.ones((da, da))) / da
    return jnp.asarray(m, BF16)


def _qk_normed(x, gmat, gain):
    sq = x * x
    hi = sq.astype(BF16)
    lo = (sq - hi.astype(F32)).astype(BF16)
    msq = _dot(hi, gmat) + _dot(lo, gmat)
    return x * lax.rsqrt(msq + EPS) * gain


def _rotated(x, cos, sin_signed, da):
    q = da // 4
    width = x.shape[-1]
    lane = lax.broadcasted_iota(jnp.int32, x.shape, 1)
    first = (lane % (2 * q)) < q
    partner = jnp.where(first, pltpu.roll(x, width - q, axis=1), pltpu.roll(x, q, axis=1))
    return x * cos + partner * sin_signed


def _attn_prep_kernel(*refs, rope, emit_f32, da):
    if rope:
        q_ref, k_ref, v_ref, gm_ref, gn_ref, cos_ref, sin_ref = refs[:7]
        outs = refs[7:]
    else:
        q_ref, k_ref, v_ref, gm_ref, gn_ref = refs[:5]
        outs = refs[5:]
    gm = gm_ref[...]
    w = 2 * da
    for h in range(q_ref.shape[1] // w):
        hc = slice(h * w, (h + 1) * w)
        q = _qk_normed(q_ref[:, hc], gm, gn_ref[0:1, :])
        k = _qk_normed(k_ref[:, hc], gm, gn_ref[1:2, :])
        if rope:
            q = _rotated(q, cos_ref[...], sin_ref[...], da)
            k = _rotated(k, cos_ref[...], sin_ref[...], da)
        v = v_ref[:, hc]
        outs[0][h] = (q * (math.log2(math.e) * da ** -0.5)).T.astype(BF16)
        outs[1][h] = k.astype(BF16)
        outs[2][h] = v.T.astype(BF16)
        if emit_f32:
            outs[3][h] = k
            outs[4][h] = v


def _attn_prep(proj, cols, gains, rope_tabs, nb, t, heads, da, row0, emit_f32):
    w = 2 * da
    tt = _pick_tile(t, (256, 128))
    nblk = t // tt
    rb0 = row0 // tt
    hw = heads * w
    assert all(c % hw == 0 for c in cols)
    rope = rope_tabs is not None

    def col(c0):
        return pl.BlockSpec((tt, hw), lambda b, i: (rb0 + b * nblk + i, c0 // hw))

    in_specs = [col(cols[0]), col(cols[1]), col(cols[2]),
                pl.BlockSpec((w, w), lambda b, i: (0, 0)),
                pl.BlockSpec((2, w), lambda b, i: (0, 0))]
    args = [proj, proj, proj, _half_mean_matrix(da), gains]
    if rope:
        in_specs += [pl.BlockSpec((tt, w), lambda b, i: (i, 0))] * 2
        args += list(rope_tabs)
    rows = pl.BlockSpec((None, heads, tt, w), lambda b, i: (b, 0, i, 0))
    cols_t = pl.BlockSpec((None, heads, w, tt), lambda b, i: (b, 0, 0, i))
    out_specs = [cols_t, rows, cols_t]
    out_shape = [jax.ShapeDtypeStruct((nb, heads, w, t), BF16),
                 jax.ShapeDtypeStruct((nb, heads, t, w), BF16),
                 jax.ShapeDtypeStruct((nb, heads, w, t), BF16)]
    if emit_f32:
        out_specs += [rows, rows]
        out_shape += [jax.ShapeDtypeStruct((nb, heads, t, w), F32)] * 2
    return pl.pallas_call(
        functools.partial(_attn_prep_kernel, rope=rope, emit_f32=emit_f32, da=da),
        grid=(nb, nblk),
        in_specs=in_specs,
        out_specs=out_specs,
        out_shape=out_shape,
        compiler_params=_params(("parallel", "parallel")),
    )(*args)


def _attn_kernel(qt_ref, k_ref, vt_ref, lam_ref, g_ref, carry_ref, o_ref, *, tk, lam_init, da):
    del carry_ref
    nk = k_ref.shape[0] // tk
    qt = qt_ref[...]
    feat = lax.broadcasted_iota(jnp.int32, qt.shape, 0)
    zero = jnp.zeros_like(qt)

    tq = qt.shape[1]
    q2t = jnp.concatenate([jnp.where(feat < da, qt, zero), jnp.where(feat >= da, qt, zero)], axis=1)

    def scores(i):
        return _dot(k_ref[i * tk:(i + 1) * tk, :], q2t)

    ahead = [scores(i) for i in range(min(ATTN_LOOKAHEAD, nk))]
    m = l = acc = None
    for i in range(nk):
        s = ahead.pop(0)
        if i + ATTN_LOOKAHEAD < nk:
            ahead.append(scores(i + ATTN_LOOKAHEAD))
        m_tile = jnp.max(s, axis=0, keepdims=True)
        m_new = m_tile if i == 0 else jnp.maximum(m, m_tile)
        p = jnp.exp2(s - m_new)
        pv = _dot(vt_ref[:, i * tk:(i + 1) * tk], p.astype(BF16))
        if i == 0:
            l, acc = jnp.sum(p, axis=0, keepdims=True), pv
        else:
            alpha = jnp.exp2(m - m_new)
            l = alpha * l + jnp.sum(p, axis=0, keepdims=True)
            acc = alpha * acc + pv
        m = m_new
    o = acc / l
    lam_p = lam_ref[...]
    lam = (jnp.exp(jnp.sum(lam_p[0:1, :] * lam_p[1:2, :], axis=-1, keepdims=True))
           - jnp.exp(jnp.sum(lam_p[2:3, :] * lam_p[3:4, :], axis=-1, keepdims=True)) + lam_init)
    o = o[:, :tq] - lam * o[:, tq:]
    o = o * lax.rsqrt(jnp.mean(o * o, axis=0, keepdims=True) + EPS) * g_ref[...]
    o_ref[...] = (o * (1.0 - lam_init)).T.astype(BF16)


def _attention(qt, k, vt, lam_p, gain, lam_init, da, *, out_row0, carry):
    nb, heads, w, n = qt.shape
    m = k.shape[2]
    tq = _pick_tile(n, (256, 128))
    tk = _pick_tile(m, (256, 128))
    nq = n // tq
    orb = out_row0 // tq
    return pl.pallas_call(
        functools.partial(_attn_kernel, tk=tk, lam_init=lam_init, da=da),
        grid=(nb, heads, nq),
        in_specs=[pl.BlockSpec((None, None, w, tq), lambda b, h, i: (b, h, 0, i)),
                  pl.BlockSpec((None, None, m, w), lambda b, h, i: (b, h, 0, 0)),
                  pl.BlockSpec((None, None, w, m), lambda b, h, i: (b, h, 0, 0)),
                  pl.BlockSpec(lam_p.shape, lambda b, h, i: (0, 0)),
                  pl.BlockSpec((w, 1), lambda b, h, i: (0, 0)),
                  pl.BlockSpec(memory_space=pl.ANY)],
        out_specs=pl.BlockSpec((tq, w), lambda b, h, i: (orb + b * nq + i, h)),
        out_shape=jax.ShapeDtypeStruct(carry.shape, BF16),
        input_output_aliases={5: 0},
        compiler_params=_params(("parallel", "parallel", "parallel")),
    )(qt, k, vt, lam_p, gain, carry)


def _scan_constants(c, reverse):
    i = np.arange(c)[:, None]
    t = np.arange(c)[None, :]

    levels, masks = [], []
    g = c
    while g >= 2:
        h = g // 2
        mat = np.zeros((c, c), np.float32)
        for r in range(c):
            mid = (r // g) * g + h
            if not reverse:
                if r >= mid:
                    mat[r, mid:r + 1] = 1.0
                else:
                    mat[r, r + 1:mid] = 1.0
            else:
                if r < mid:
                    mat[r, r:mid] = 1.0
                else:
                    mat[r, mid:r] = 1.0
        same = (i // g) == (t // g)
        if not reverse:
            ok = same & ((i % g) >= h) & ((t % g) < h)
        else:
            ok = same & ((i % g) < h) & ((t % g) >= h)
        levels.append(mat)
        masks.append(ok.astype(np.float32))
        g = h
    masks.append(np.eye(c, dtype=np.float32))
    return jnp.asarray(np.concatenate(levels, axis=0), BF16), jnp.asarray(np.stack(masks), F32)


def _chunk_cumsum(x, c, reverse):
    tb = x.shape[0]
    pos = lax.broadcasted_iota(jnp.int32, x.shape, 0) % c
    s = 1
    while s < c:
        if reverse:
            x = x + jnp.where(pos < c - s, pltpu.roll(x, tb - s, axis=0), 0.0)
        else:
            x = x + jnp.where(pos >= s, pltpu.roll(x, s, axis=0), 0.0)
        s *= 2
    return x


def _scan_block(q, k, v, la, slow_ref, mask_ref, st_ref, u_ref, o_ref, *, c, reverse):
    tb = q.shape[0]
    n = tb // c
    hb, dv, dk = st_ref.shape
    chunks = [slice(ci * c, (ci + 1) * c) for ci in range(n)]
    kcols = [slice(j * dk, (j + 1) * dk) for j in range(hb)]
    vcols = [slice(j * dv, (j + 1) * dv) for j in range(hb)]
    ref_row = c // 2 if reverse else c // 2 - 1
    end_row = 0 if reverse else c - 1
    b = _chunk_cumsum(la, c, reverse)
    b_ref = jnp.concatenate([jnp.broadcast_to(b[sl][ref_row:ref_row + 1], (c, b.shape[1])) for sl in chunks], axis=0)
    b_end = jnp.concatenate([jnp.broadcast_to(b[sl][end_row:end_row + 1], (c, b.shape[1])) for sl in chunks], axis=0)
    zq = b - b_ref
    safe = jnp.max(jnp.abs(zq)) < SAFE_EXPONENT
    q_in = (q * jnp.exp(b)).astype(BF16)
    k_up = (k * jnp.exp(b_end - b)).astype(BF16)
    v_bf = v.astype(BF16)
    row = lax.broadcasted_iota(jnp.int32, (c, c), 0)
    col = lax.broadcasted_iota(jnp.int32, (c, c), 1)
    visible = (col >= row) if reverse else (col <= row)
    n_levels = slow_ref.shape[0] // c

    def fast_scores():
        q_mid = (q * jnp.exp(zq)).astype(BF16)
        k_mid = (k * jnp.exp(-zq)).astype(BF16)
        return jnp.concatenate([jnp.where(visible, _dot_nt(q_mid[sl, kc], k_mid[sl, kc]), 0.0)
                                for kc in kcols for sl in chunks], axis=0)

    def slow_scores():
        q_bf = q.astype(BF16)
        k_bf = k.astype(BF16)
        parts = _split3(la)
        out = []
        for kc in kcols:
            for sl in chunks:
                zs = _dot_exact_lhs(slow_ref[...], tuple(p[sl, kc] for p in parts))
                a = mask_ref[n_levels] * _dot_nt(q_bf[sl, kc], k_bf[sl, kc])
                for lv in range(n_levels):
                    e = jnp.exp(zs[lv * c:(lv + 1) * c])
                    a += mask_ref[lv] * _dot_nt((q[sl, kc] * e).astype(BF16), (k[sl, kc] * e).astype(BF16))
                out.append(a)
        return jnp.concatenate(out, axis=0)

    a = lax.cond(safe, fast_scores, slow_scores).astype(BF16)
    for j in range(hb):
        for ci, sl in enumerate(chunks):
            u_ref[j * n + ci] = _dot_tn(v_bf[sl, vcols[j]], k_up[sl, kcols[j]])
    entry = {}
    for j in range(hb):
        st = st_ref[j]
        for ci in (range(n - 1, -1, -1) if reverse else range(n)):
            entry[j, ci] = st.astype(BF16)
            st = st * jnp.exp(b_end[chunks[ci], kcols[j]][0:1]) + u_ref[j * n + ci]
        st_ref[j] = st
    for j in range(hb):
        for ci, sl in enumerate(chunks):
            rows = slice(j * tb + ci * c, j * tb + (ci + 1) * c)
            o_ref[sl, vcols[j]] = _dot_nt(q_in[sl, kcols[j]], entry[j, ci]) + _dot(a[rows], v_bf[sl, vcols[j]])


def _scan_finish(o_ref, prev_ref, gate_ref, gn_ref, y_ref):
    dv = gn_ref.shape[1]
    o = o_ref[...] + prev_ref[...]
    gate = _silu(gate_ref[...])
    for j in range(o.shape[1] // dv):
        cols = slice(j * dv, (j + 1) * dv)
        oj = o[:, cols]
        oj = oj * lax.rsqrt(jnp.mean(oj * oj, axis=-1, keepdims=True) + EPS) * gn_ref[...]
        y_ref[:, cols] = (oj * gate[:, cols]).astype(BF16)


def _scan_state_io(s0_ref, sT_ref, st_ref, do_body):
    t = pl.program_id(2)
    hb = st_ref.shape[0]

    @pl.when(t == 0)
    def _():
        for j in range(hb):
            st_ref[j] = s0_ref[j].T

    do_body()

    @pl.when(t == pl.num_programs(2) - 1)
    def _():
        for j in range(hb):
            sT_ref[j] = st_ref[j].T


def _without(refs, pos):
    return refs if pos is None else refs[:pos] + refs[pos + 1:]


def _hgrn_kernel(*refs, c, reverse, last_pass, carry_pos):
    refs = _without(refs, carry_pos)
    if last_pass:
        (q_ref, f_ref, i_ref, lb_ref, s0_ref, slow_ref, mask_ref, prev_ref, gate_ref, gn_ref,
         y_ref, sT_ref, st_ref, u_ref, o_scr) = refs
    else:
        (q_ref, f_ref, i_ref, lb_ref, s0_ref, slow_ref, mask_ref,
         y_ref, sT_ref, st_ref, u_ref) = refs
        o_scr = y_ref

    def body():
        lb = lb_ref[...]
        f = lb + (1.0 - lb) * _sigmoid(f_ref[...])
        _scan_block(_silu(q_ref[...]), 1.0 - f, i_ref[...], jnp.log(f),
                    slow_ref, mask_ref, st_ref, u_ref, o_scr, c=c, reverse=reverse)
        if last_pass:
            _scan_finish(o_scr, prev_ref, gate_ref, gn_ref, y_ref)

    _scan_state_io(s0_ref, sT_ref, st_ref, body)


def _gla_kernel(*refs, c, reverse, last_pass, carry_pos, rank, q_scale):
    refs = _without(refs, carry_pos)
    if last_pass:
        (q_ref, k_ref, v_ref, r_ref, wd_ref, bd_ref, s0_ref, slow_ref, mask_ref,
         prev_ref, gate_ref, gn_ref, y_ref, sT_ref, st_ref, u_ref, o_scr) = refs
    else:
        (q_ref, k_ref, v_ref, r_ref, wd_ref, bd_ref, s0_ref, slow_ref, mask_ref,
         y_ref, sT_ref, st_ref, u_ref) = refs
        o_scr = y_ref

    def body():
        lo = rank if reverse else 0
        r = r_ref[...][:, lo:lo + rank].astype(BF16)
        logit = _dot(r, wd_ref[...].astype(BF16)) + bd_ref[...]
        la = (jnp.minimum(logit, 0.0) - jnp.log(1.0 + jnp.exp(-jnp.abs(logit)))) / GLA_TAU
        _scan_block(q_ref[...] * q_scale, k_ref[...], v_ref[...], la,
                    slow_ref, mask_ref, st_ref, u_ref, o_scr, c=c, reverse=reverse)
        if last_pass:
            _scan_finish(o_scr, prev_ref, gate_ref, gn_ref, y_ref)

    _scan_state_io(s0_ref, sT_ref, st_ref, body)


def _scan_call(kernel, proj, col_specs, extra, s0, prev, gate_col, gn, nb, t, heads, dk, dv, row0,
               reverse, out_row0=0, out_rows=None, carry=None, **kw):
    tb = _pick_tile(t, (SCAN_BLOCK, 128))
    c = min(SCAN_CHUNK, tb)
    nblk = t // tb
    rb0 = row0 // tb
    last_pass = prev is not None
    slow, masks = _scan_constants(c, reverse)

    def tblk(i):
        return (nblk - 1 - i) if reverse else i

    hb = _pick_tile(heads, (SCAN_HEADS, 2, 1))

    def col(off, width, per_head=True):
        if not per_head:
            return pl.BlockSpec((tb, width), lambda b, h, i: (rb0 + b * nblk + tblk(i), off // width))
        assert off % (hb * width) == 0
        return pl.BlockSpec((tb, hb * width),
                            lambda b, h, i: (rb0 + b * nblk + tblk(i), off // (hb * width) + h))

    in_specs = [col(*spec[1:]) for spec in col_specs]
    args = [spec[0] for spec in col_specs]
    for e in extra:
        in_specs.append(pl.BlockSpec((e.shape[0], hb * dk), lambda b, h, i: (0, h)))
        args.append(e)
    in_specs.append(pl.BlockSpec((None, hb, dk, dv), lambda b, h, i: (b, h, 0, 0)))
    args.append(s0)
    for cst in (slow, masks):
        in_specs.append(pl.BlockSpec(cst.shape, lambda b, h, i, nd=cst.ndim: (0,) * nd))
        args.append(cst)
    scratch = [pltpu.VMEM((hb, dv, dk), F32), pltpu.VMEM((hb * (tb // c), dv, dk), F32)]
    if last_pass:
        in_specs += [pl.BlockSpec((tb, hb * dv), lambda b, h, i: (b * nblk + tblk(i), h)),
                     col(gate_col, dv),
                     pl.BlockSpec((1, dv), lambda b, h, i: (0, 0))]
        args += [prev, proj, gn]
        scratch.append(pltpu.VMEM((tb, hb * dv), F32))
    y_dtype = BF16 if last_pass else F32
    orb = out_row0 // tb
    out_rows = nb * t if out_rows is None else out_rows
    aliases, carry_pos = {}, None
    if carry is not None:
        carry_pos = len(args)
        aliases = {carry_pos: 0}
        in_specs.append(pl.BlockSpec(memory_space=pl.ANY))
        args.append(carry)
    return pl.pallas_call(
        functools.partial(kernel, c=c, reverse=reverse, last_pass=last_pass, carry_pos=carry_pos, **kw),
        grid=(nb, heads // hb, nblk),
        in_specs=in_specs,
        out_specs=[pl.BlockSpec((tb, hb * dv), lambda b, h, i: (orb + b * nblk + tblk(i), h)),
                   pl.BlockSpec((None, hb, dk, dv), lambda b, h, i: (b, h, 0, 0))],
        out_shape=[jax.ShapeDtypeStruct((out_rows, heads * dv), y_dtype),
                   jax.ShapeDtypeStruct((nb, heads, dk, dv), F32)],
        input_output_aliases=aliases,
        scratch_shapes=scratch,
        compiler_params=_params(("parallel", "parallel", "arbitrary")),
    )(*args)


class _Geometry:
    pass


def _rope_tables(n_tok, da):
    ax = da // 2
    rows = n_tok // GRID_W
    r = jnp.broadcast_to(jnp.arange(rows, dtype=F32)[:, None], (rows, GRID_W)).reshape(-1)
    col = jnp.broadcast_to(jnp.arange(GRID_W, dtype=F32)[None, :], (rows, GRID_W)).reshape(-1)
    inv = ROPE_BASE ** (-jnp.arange(0, ax, 2, dtype=F32) / ax)
    ar, ac = r[:, None] * inv, col[:, None] * inv
    ang = jnp.concatenate([ar, ar, ac, ac], axis=-1)
    cos, sin = jnp.cos(ang), jnp.sin(ang)
    q = da // 4
    sign = jnp.where((jnp.arange(da) % (2 * q)) < q, -1.0, 1.0).astype(F32)
    return jnp.tile(cos, (1, 2)), jnp.tile(sin * sign, (1, 2))


def kernel(x_prompt, x_sample, c, cache_attn_k, cache_attn_v, state_hgrn, state_gla, c_ctx, ada_w, ada_b, norm_w, ffn_w_gate, ffn_w_up, ffn_w_down, w_in, attn_qk_norm, attn_lambda, attn_out_norm, hgrn_lb_logits, hgrn_out_norm, gla_w_decay, gla_b_decay, gla_out_norm, w_branch, w_out):
    batch, seq, d = x_prompt.shape
    dec_batch, dec_seq, _ = x_sample.shape
    depth = ada_w.shape[0]
    ha, da = cache_attn_k.shape[2], cache_attn_k.shape[5]
    dva = 2 * da
    hh, hdk, hdv = state_hgrn.shape[3:]
    hg, gdk, gdv = state_gla.shape[3:]
    rank = gla_w_decay.shape[2]
    sizes = (ha * 2 * da, ha * 2 * da, ha * dva, hh * hdk, hh * hdk, hh * hdk, hh * hdv, hh * hdv,
             hg * gdk, hg * gdk, hg * gdv, hg * gdv, 2 * rank, 3 * d)
    assert sum(sizes) == w_in.shape[2] and hdk == LANES and hdv == LANES and gdk == LANES and dva == LANES

    geo = _Geometry()
    geo.n_ctx_rows = batch * seq
    geo.dec_seq = dec_seq
    geo.tile_rows = math.gcd(geo.n_ctx_rows, dec_seq)
    n_ctx = geo.n_ctx_rows

    src = [0] + [int(v) for v in np.cumsum(sizes)]
    n_main = src[12]
    names = ('aq', 'ak', 'av', 'hq', 'hf_f', 'hf_b', 'hi', 'hg', 'gq', 'gk', 'gv', 'gg')
    off = {nm: src[i] for i, nm in enumerate(names)}

    n_cond = 1 + dec_batch
    cond = jnp.concatenate([c_ctx[None], c, jnp.zeros((-n_cond % 8, d), F32)], axis=0)
    mod = _adaln(cond, ada_w, ada_b).reshape(depth, cond.shape[0], 9, d)

    lb_w = jax.nn.softmax(hgrn_lb_logits.astype(F32), axis=0)
    lower_bounds = jnp.cumsum(lb_w, axis=0) - lb_w[0]
    rope_tabs = _rope_tables(dec_seq, da)

    n_lat = dec_batch * dec_seq
    nt = n_ctx + n_lat
    zeros_h = jnp.zeros((batch, hh, hdk, hdv), F32)
    zeros_g = jnp.zeros((batch, hg, gdk, gdv), F32)
    ks_l, vs_l, sh_l, sg_l = [], [], [], []
    x = None
    wg_bf, wu_bf, wd_bf = ffn_w_gate.astype(BF16), ffn_w_up.astype(BF16), ffn_w_down.astype(BF16)
    wb_bf, wo_bf = w_branch.astype(BF16), w_out.astype(BF16)
    cache_k = cache_attn_k.reshape(cache_attn_k.shape[:4] + (2 * da,))
    br_a = br_h = br_g = None

    for l in range(depth):
        m_l = mod[l]
        nw = norm_w[l]
        ffn1 = (m_l, nw[0:1], wg_bf, wu_bf, wd_bf, (l, 0), 0, geo)
        if l == 0:
            x = jnp.concatenate([x_prompt.reshape(n_ctx, d), x_sample.reshape(n_lat, d)], axis=0)
            br_a = jnp.zeros((nt, ha * dva), BF16)
            br_h = jnp.zeros((nt, hh * hdv), BF16)
            br_g = jnp.zeros((nt, hg * gdv), BF16)
        x, h_mix = _ffn(x, *ffn1, nxt=(nw[1:2], 3))
        w_l = w_in[l]
        proj = _matmul(h_mix, w_l[:, :n_main].astype(BF16), F32)
        w_glr = jnp.pad(w_l[:, src[12]:src[13]], ((0, 0), (0, LANES - 2 * rank))).astype(BF16)
        glr = _matmul(h_mix, w_glr, F32)
        gates = _matmul(h_mix, w_l[:, src[13]:].astype(BF16), BF16, gate=True)

        lam_init = 0.8 - 0.6 * math.exp(-0.3 * l)
        gains = jnp.tile(attn_qk_norm[l], (1, 2))
        cols = (off['aq'], off['ak'], off['av'])
        gn_a = attn_out_norm[l][:, None]
        qp, kp, vp, k32, v32 = _attn_prep(proj, cols, gains, None, batch, seq, ha, da, 0, True)
        ks_l.append(k32.reshape(batch, ha, seq, 2, da))
        vs_l.append(v32)
        qs, ksm, vsm = _attn_prep(proj, cols, gains, rope_tabs, dec_batch, dec_seq, ha, da, n_ctx, False)
        attn = (attn_lambda[l], gn_a, lam_init, da)
        k_all = jnp.concatenate([cache_k[:, l].astype(BF16), ksm], axis=2)
        v_all = jnp.concatenate([jnp.swapaxes(cache_attn_v[:, l], 2, 3).astype(BF16), vsm], axis=3)
        br_a = _attention(qp, kp, vp, *attn, out_row0=0, carry=br_a)
        br_a = _attention(qs, k_all, v_all, *attn, out_row0=n_ctx, carry=br_a)

        lb = lower_bounds[l][:, None]
        gn_h = hgrn_out_norm[l][None]

        def hgrn(nb, t, row0, s0, carry):
            specs_b = ((proj, off['hq'], hdk), (proj, off['hf_b'], hdk), (proj, off['hi'], hdv))
            specs_f = ((proj, off['hq'], hdk), (proj, off['hf_f'], hdk), (proj, off['hi'], hdv))
            o_b, s_b = _scan_call(_hgrn_kernel, proj, specs_b, [lb[1]], s0[1], None, None, None,
                                  nb, t, hh, hdk, hdv, row0, True)
            y, s_f = _scan_call(_hgrn_kernel, proj, specs_f, [lb[0]], s0[0], o_b, off['hg'], gn_h,
                                nb, t, hh, hdk, hdv, row0, False, out_row0=row0, out_rows=nt, carry=carry)
            return y, jnp.stack([s_f, s_b], axis=1)

        br_h, s_h = hgrn(batch, seq, 0, (zeros_h, zeros_h), br_h)
        br_h, _ = hgrn(dec_batch, dec_seq, n_ctx, (state_hgrn[:, l, 0], state_hgrn[:, l, 1]), br_h)
        sh_l.append(s_h)

        wd = gla_w_decay[l]
        bd = gla_b_decay[l][:, None]
        gn_g = gla_out_norm[l][None]

        def gla(nb, t, row0, s0, carry):
            specs = ((proj, off['gq'], gdk), (proj, off['gk'], gdk), (proj, off['gv'], gdv),
                     (glr, 0, LANES, False))
            kw = dict(rank=rank, q_scale=gdk ** -0.5)
            o_b, s_b = _scan_call(_gla_kernel, proj, specs, [wd[1], bd[1]], s0[1], None, None, None,
                                  nb, t, hg, gdk, gdv, row0, True, **kw)
            y, s_f = _scan_call(_gla_kernel, proj, specs, [wd[0], bd[0]], s0[0], o_b, off['gg'], gn_g,
                                nb, t, hg, gdk, gdv, row0, False, out_row0=row0, out_rows=nt, carry=carry,
                                **kw)
            return y, jnp.stack([s_f, s_b], axis=1)

        br_g, s_g = gla(batch, seq, 0, (zeros_g, zeros_g), br_g)
        br_g, _ = gla(dec_batch, dec_seq, n_ctx, (state_gla[:, l, 0], state_gla[:, l, 1]), br_g)
        sg_l.append(s_g)

        x = _mix_out(br_a, br_h, br_g, gates, wb_bf, wo_bf, l, x, m_l, 5, geo)
        ffn2 = (m_l, nw[2:3], wg_bf, wu_bf, wd_bf, (l, 1), 6, geo)
        if l + 1 < depth:
            x = _ffn(x, *ffn2)[0]
        else:
            y_prompt = _ffn(x, *ffn2, rows=n_ctx)[0].reshape(batch, seq, d)
            y_sample = _ffn(x, *ffn2, rows=n_lat, in_row0=n_ctx)[0].reshape(dec_batch, dec_seq, d)

    return (y_prompt, y_sample, jnp.stack(ks_l, axis=1), jnp.stack(vs_l, axis=1),
            jnp.stack(sh_l, axis=1), jnp.stack(sg_l, axis=1))
```

```python
import functools
import math

import jax
import jax.numpy as jnp
import numpy as np
from jax import lax
from jax.experimental import pallas as pl
from jax.experimental.pallas import tpu as pltpu

F32 = jnp.float32
BF16 = jnp.bfloat16

GRID_W = 64
GLA_TAU = 16.0
ROPE_BASE = 10000.0
EPS = 1e-6
LANES = 128
SCAN_CHUNK = 64
SCAN_BLOCK = 512
SCAN_HEADS = 4
ATTN_LOOKAHEAD = 2
SAFE_EXPONENT = 80.0
VMEM_LIMIT = 56 * 1024 * 1024


def _dot(a, b):
    return jnp.dot(a, b, preferred_element_type=F32)


def _dot_nt(a, b):
    return lax.dot_general(a, b, (((1,), (1,)), ((), ())), preferred_element_type=F32)


def _dot_tn(a, b):
    return lax.dot_general(a, b, (((0,), (0,)), ((), ())), preferred_element_type=F32)


def _sigmoid(x):
    return 1.0 / (1.0 + jnp.exp(-x))


def _silu(x):
    return x * _sigmoid(x)


def _split3(x):
    hi = x.astype(BF16)
    r1 = x - hi.astype(F32)
    mid = r1.astype(BF16)
    lo = (r1 - mid.astype(F32)).astype(BF16)
    return hi, mid, lo


def _dot_exact_lhs(m, parts):
    return _dot(m, parts[0]) + _dot(m, parts[1]) + _dot(m, parts[2])


def _pick_tile(n, candidates):
    for c in candidates:
        if n % c == 0:
            return c
    return n


def _params(sem):
    return pltpu.CompilerParams(dimension_semantics=sem, vmem_limit_bytes=VMEM_LIMIT)


def _adaln_kernel(c_ref, w_ref, b_ref, o_ref):
    s = _silu(c_ref[...]).astype(BF16)
    o_ref[...] = _dot(s, w_ref[...].astype(BF16)) + b_ref[...]


def _adaln(cond, ada_w, ada_b):
    depth, d, n = ada_w.shape
    rows = cond.shape[0]
    tn = _pick_tile(n, (1024, 512, 256, 128))
    return pl.pallas_call(
        _adaln_kernel,
        grid=(depth, n // tn),
        in_specs=[pl.BlockSpec((rows, d), lambda l, j: (0, 0)),
                  pl.BlockSpec((None, d, tn), lambda l, j: (l, 0, j)),
                  pl.BlockSpec((None, 1, tn), lambda l, j: (l, 0, j))],
        out_specs=pl.BlockSpec((None, rows, tn), lambda l, j: (l, 0, j)),
        out_shape=jax.ShapeDtypeStruct((depth, rows, n), F32),
        compiler_params=_params(("parallel", "parallel")),
    )(cond, ada_w, ada_b.reshape(depth, 1, n))


def _modulated(x, nw, shift, scale):
    y = x * lax.rsqrt(jnp.mean(x * x, axis=-1, keepdims=True) + EPS) * nw
    return y * (1.0 + scale) + shift


def _ffn_kernel(*refs, base, next_base):
    x_ref, m_ref, nw_ref, wg_ref, wu_ref, wd_ref = refs[:6]
    if next_base is not None:
        nw_next_ref, o_ref, h_next_ref, h_ref = refs[6:]
    else:
        o_ref, h_ref = refs[6:]
    j = pl.program_id(1)

    @pl.when(j == 0)
    def _():
        h = _modulated(x_ref[...], nw_ref[...], m_ref[base:base + 1, :], m_ref[base + 1:base + 2, :])
        h_ref[...] = h.astype(BF16)
        o_ref[...] = jnp.zeros_like(o_ref)

    h = h_ref[...]
    g = _dot(h, wg_ref[...])
    u = _dot(h, wu_ref[...])
    o_ref[...] += _dot((_silu(g) * u).astype(BF16), wd_ref[...])

    @pl.when(j == pl.num_programs(1) - 1)
    def _():
        y = x_ref[...] + 0.5 * m_ref[base + 2:base + 3, :] * o_ref[...]
        o_ref[...] = y
        if next_base is not None:
            h_next_ref[...] = _modulated(y, nw_next_ref[...], m_ref[next_base:next_base + 1, :],
                                         m_ref[next_base + 1:next_base + 2, :]).astype(BF16)


def _cond_index(i, tm, row0, n_ctx_rows, seq_rows):
    r = row0 + i * tm
    return jnp.where(r < n_ctx_rows, 0, (r - n_ctx_rows) // seq_rows + 1)


def _ffn(x, mod, nw, wg, wu, wd, wsel, base, geo, *, rows=None, in_row0=0, nxt=None):
    d = x.shape[1]
    rows = x.shape[0] if rows is None else rows
    f = wg.shape[-1]
    tm = _pick_tile(geo.tile_rows, (512, 256, 128))
    tf = _pick_tile(f, (512, 256, 128))
    cidx = functools.partial(_cond_index, tm=tm, row0=in_row0, n_ctx_rows=geo.n_ctx_rows,
                             seq_rows=geo.dec_seq)
    ib = in_row0 // tm
    l, s = wsel
    in_specs = [pl.BlockSpec((tm, d), lambda i, j: (i + ib, 0)),
                pl.BlockSpec((None, 9, d), lambda i, j: (cidx(i), 0, 0)),
                pl.BlockSpec((1, d), lambda i, j: (0, 0)),
                pl.BlockSpec((None, None, d, tf), lambda i, j: (l, s, 0, j)),
                pl.BlockSpec((None, None, d, tf), lambda i, j: (l, s, 0, j)),
                pl.BlockSpec((None, None, tf, d), lambda i, j: (l, s, j, 0))]
    args = [x, mod, nw, wg, wu, wd]
    out_specs = [pl.BlockSpec((tm, d), lambda i, j: (i, 0))]
    out_shape = [jax.ShapeDtypeStruct((rows, d), F32)]
    if nxt is not None:
        in_specs.append(pl.BlockSpec((1, d), lambda i, j: (0, 0)))
        args.append(nxt[0])
        out_specs.append(pl.BlockSpec((tm, d), lambda i, j: (i, 0)))
        out_shape.append(jax.ShapeDtypeStruct((rows, d), BF16))
    return pl.pallas_call(
        functools.partial(_ffn_kernel, base=base, next_base=None if nxt is None else nxt[1]),
        grid=(rows // tm, f // tf),
        in_specs=in_specs,
        out_specs=out_specs,
        out_shape=out_shape,
        scratch_shapes=[pltpu.VMEM((tm, d), BF16)],
        compiler_params=_params(("parallel", "arbitrary")),
    )(*args)


def _matmul_kernel(a_ref, w_ref, o_ref, *, gate):
    y = _dot(a_ref[...], w_ref[...])
    o_ref[...] = (_sigmoid(y) if gate else y).astype(o_ref.dtype)


def _matmul(a, w, out_dtype, gate=False):
    nt, k = a.shape
    n = w.shape[1]
    tm = _pick_tile(nt, (1024, 512, 256, 128))
    tn = _pick_tile(n, (1024, 768, 512, 256, 128))
    return pl.pallas_call(
        functools.partial(_matmul_kernel, gate=gate),
        grid=(nt // tm, n // tn),
        in_specs=[pl.BlockSpec((tm, k), lambda i, j: (i, 0)),
                  pl.BlockSpec((k, tn), lambda i, j: (0, j))],
        out_specs=pl.BlockSpec((tm, tn), lambda i, j: (i, j)),
        out_shape=jax.ShapeDtypeStruct((nt, n), out_dtype),
        compiler_params=_params(("parallel", "parallel")),
    )(a, w)


def _mix_out_kernel(ba_ref, bh_ref, bg_ref, g_ref, wb_ref, wo_ref, x_ref, m_ref, o_ref, *, row):
    d = x_ref.shape[1]
    acc = g_ref[:, 0:d].astype(F32) * _dot(ba_ref[...], wb_ref[0])
    acc += g_ref[:, d:2 * d].astype(F32) * _dot(bh_ref[...], wb_ref[1])
    acc += g_ref[:, 2 * d:3 * d].astype(F32) * _dot(bg_ref[...], wb_ref[2])
    o_ref[...] = x_ref[...] + m_ref[row:row + 1, :] * _dot(acc.astype(BF16), wo_ref[...])


def _mix_out(br_a, br_h, br_g, gates, wb, wo, layer, x, mod, row, geo):
    nt, bw = br_a.shape
    d = x.shape[1]
    tm = _pick_tile(geo.tile_rows, (256, 128))
    cidx = functools.partial(_cond_index, tm=tm, row0=0, n_ctx_rows=geo.n_ctx_rows, seq_rows=geo.dec_seq)
    bspec = pl.BlockSpec((tm, bw), lambda i: (i, 0))
    return pl.pallas_call(
        functools.partial(_mix_out_kernel, row=row),
        grid=(nt // tm,),
        in_specs=[bspec, bspec, bspec,
                  pl.BlockSpec((tm, 3 * d), lambda i: (i, 0)),
                  pl.BlockSpec((None, 3, bw, d), lambda i: (layer, 0, 0, 0), pipeline_mode=pl.Buffered(1)),
                  pl.BlockSpec((None, d, d), lambda i: (layer, 0, 0), pipeline_mode=pl.Buffered(1)),
                  pl.BlockSpec((tm, d), lambda i: (i, 0)),
                  pl.BlockSpec((None, 9, d), lambda i: (cidx(i), 0, 0))],
        out_specs=pl.BlockSpec((tm, d), lambda i: (i, 0)),
        out_shape=jax.ShapeDtypeStruct((nt, d), F32),
        compiler_params=_params(("parallel",)),
    )(br_a, br_h, br_g, gates, wb, wo, x, mod)


def _half_mean_matrix(da):
    m = np.kron(np.eye(2), np.ones((da, da))) / da
    return jnp.asarray(m, BF16)


def _qk_normed(x, gmat, gain):
    sq = x * x
    hi = sq.astype(BF16)
    lo = (sq - hi.astype(F32)).astype(BF16)
    msq = _dot(hi, gmat) + _dot(lo, gmat)
    return x * lax.rsqrt(msq + EPS) * gain


def _rotated(x, cos, sin_signed, da):
    q = da // 4
    width = x.shape[-1]
    lane = lax.broadcasted_iota(jnp.int32, x.shape, 1)
    first = (lane % (2 * q)) < q
    partner = jnp.where(first, pltpu.roll(x, width - q, axis=1), pltpu.roll(x, q, axis=1))
    return x * cos + partner * sin_signed


def _attn_prep_kernel(*refs, rope, emit_f32, da):
    if rope:
        q_ref, k_ref, v_ref, gm_ref, gn_ref, cos_ref, sin_ref = refs[:7]
        outs = refs[7:]
    else:
        q_ref, k_ref, v_ref, gm_ref, gn_ref = refs[:5]
        outs = refs[5:]
    gm = gm_ref[...]
    w = 2 * da
    for h in range(q_ref.shape[1] // w):
        hc = slice(h * w, (h + 1) * w)
        q = _qk_normed(q_ref[:, hc], gm, gn_ref[0:1, :])
        k = _qk_normed(k_ref[:, hc], gm, gn_ref[1:2, :])
        if rope:
            q = _rotated(q, cos_ref[...], sin_ref[...], da)
            k = _rotated(k, cos_ref[...], sin_ref[...], da)
        v = v_ref[:, hc]
        outs[0][h] = (q * (math.log2(math.e) * da ** -0.5)).T.astype(BF16)
        outs[1][h] = k.astype(BF16)
        outs[2][h] = v.T.astype(BF16)
        if emit_f32:
            outs[3][h] = k
            outs[4][h] = v


def _attn_prep(proj, cols, gains, rope_tabs, nb, t, heads, da, row0, emit_f32):
    w = 2 * da
    tt = _pick_tile(t, (256, 128))
    nblk = t // tt
    rb0 = row0 // tt
    hw = heads * w
    assert all(c % hw == 0 for c in cols)
    rope = rope_tabs is not None

    def col(c0):
        return pl.BlockSpec((tt, hw), lambda b, i: (rb0 + b * nblk + i, c0 // hw))

    in_specs = [col(cols[0]), col(cols[1]), col(cols[2]),
                pl.BlockSpec((w, w), lambda b, i: (0, 0)),
                pl.BlockSpec((2, w), lambda b, i: (0, 0))]
    args = [proj, proj, proj, _half_mean_matrix(da), gains]
    if rope:
        in_specs += [pl.BlockSpec((tt, w), lambda b, i: (i, 0))] * 2
        args += list(rope_tabs)
    rows = pl.BlockSpec((None, heads, tt, w), lambda b, i: (b, 0, i, 0))
    cols_t = pl.BlockSpec((None, heads, w, tt), lambda b, i: (b, 0, 0, i))
    out_specs = [cols_t, rows, cols_t]
    out_shape = [jax.ShapeDtypeStruct((nb, heads, w, t), BF16),
                 jax.ShapeDtypeStruct((nb, heads, t, w), BF16),
                 jax.ShapeDtypeStruct((nb, heads, w, t), BF16)]
    if emit_f32:
        out_specs += [rows, rows]
        out_shape += [jax.ShapeDtypeStruct((nb, heads, t, w), F32)] * 2
    return pl.pallas_call(
        functools.partial(_attn_prep_kernel, rope=rope, emit_f32=emit_f32, da=da),
        grid=(nb, nblk),
        in_specs=in_specs,
        out_specs=out_specs,
        out_shape=out_shape,
        compiler_params=_params(("parallel", "parallel")),
    )(*args)


def _attn_kernel(qt_ref, k_ref, vt_ref, lam_ref, g_ref, carry_ref, o_ref, *, tk, lam_init, da):
    del carry_ref
    nk = k_ref.shape[0] // tk
    qt = qt_ref[...]
    feat = lax.broadcasted_iota(jnp.int32, qt.shape, 0)
    zero = jnp.zeros_like(qt)

    tq = qt.shape[1]
    q2t = jnp.concatenate([jnp.where(feat < da, qt, zero), jnp.where(feat >= da, qt, zero)], axis=1)

    def scores(i):
        return _dot(k_ref[i * tk:(i + 1) * tk, :], q2t)

    ahead = [scores(i) for i in range(min(ATTN_LOOKAHEAD, nk))]
    m = l = acc = None
    for i in range(nk):
        s = ahead.pop(0)
        if i + ATTN_LOOKAHEAD < nk:
            ahead.append(scores(i + ATTN_LOOKAHEAD))
        m_tile = jnp.max(s, axis=0, keepdims=True)
        m_new = m_tile if i == 0 else jnp.maximum(m, m_tile)
        p = jnp.exp2(s - m_new)
        pv = _dot(vt_ref[:, i * tk:(i + 1) * tk], p.astype(BF16))
        if i == 0:
            l, acc = jnp.sum(p, axis=0, keepdims=True), pv
        else:
            alpha = jnp.exp2(m - m_new)
            l = alpha * l + jnp.sum(p, axis=0, keepdims=True)
            acc = alpha * acc + pv
        m = m_new
    o = acc / l
    lam_p = lam_ref[...]
    lam = (jnp.exp(jnp.sum(lam_p[0:1, :] * lam_p[1:2, :], axis=-1, keepdims=True))
           - jnp.exp(jnp.sum(lam_p[2:3, :] * lam_p[3:4, :], axis=-1, keepdims=True)) + lam_init)
    o = o[:, :tq] - lam * o[:, tq:]
    o = o * lax.rsqrt(jnp.mean(o * o, axis=0, keepdims=True) + EPS) * g_ref[...]
    o_ref[...] = (o * (1.0 - lam_init)).T.astype(BF16)


def _attention(qt, k, vt, lam_p, gain, lam_init, da, *, out_row0, carry):
    nb, heads, w, n = qt.shape
    m = k.shape[2]
    tq = _pick_tile(n, (256, 128))
    tk = _pick_tile(m, (256, 128))
    nq = n // tq
    orb = out_row0 // tq
    return pl.pallas_call(
        functools.partial(_attn_kernel, tk=tk, lam_init=lam_init, da=da),
        grid=(nb, heads, nq),
        in_specs=[pl.BlockSpec((None, None, w, tq), lambda b, h, i: (b, h, 0, i)),
                  pl.BlockSpec((None, None, m, w), lambda b, h, i: (b, h, 0, 0)),
                  pl.BlockSpec((None, None, w, m), lambda b, h, i: (b, h, 0, 0)),
                  pl.BlockSpec(lam_p.shape, lambda b, h, i: (0, 0)),
                  pl.BlockSpec((w, 1), lambda b, h, i: (0, 0)),
                  pl.BlockSpec(memory_space=pl.ANY)],
        out_specs=pl.BlockSpec((tq, w), lambda b, h, i: (orb + b * nq + i, h)),
        out_shape=jax.ShapeDtypeStruct(carry.shape, BF16),
        input_output_aliases={5: 0},
        compiler_params=_params(("parallel", "parallel", "parallel")),
    )(qt, k, vt, lam_p, gain, carry)


def _scan_constants(c, reverse):
    i = np.arange(c)[:, None]
    t = np.arange(c)[None, :]

    levels, masks = [], []
    g = c
    while g >= 2:
        h = g // 2
        mat = np.zeros((c, c), np.float32)
        for r in range(c):
            mid = (r // g) * g + h
            if not reverse:
                if r >= mid:
                    mat[r, mid:r + 1] = 1.0
                else:
                    mat[r, r + 1:mid] = 1.0
            else:
                if r < mid:
                    mat[r, r:mid] = 1.0
                else:
                    mat[r, mid:r] = 1.0
        same = (i // g) == (t // g)
        if not reverse:
            ok = same & ((i % g) >= h) & ((t % g) < h)
        else:
            ok = same & ((i % g) < h) & ((t % g) >= h)
        levels.append(mat)
        masks.append(ok.astype(np.float32))
        g = h
    masks.append(np.eye(c, dtype=np.float32))
    return jnp.asarray(np.concatenate(levels, axis=0), BF16), jnp.asarray(np.stack(masks), F32)


def _chunk_cumsum(x, c, reverse):
    tb = x.shape[0]
    pos = lax.broadcasted_iota(jnp.int32, x.shape, 0) % c
    s = 1
    while s < c:
        if reverse:
            x = x + jnp.where(pos < c - s, pltpu.roll(x, tb - s, axis=0), 0.0)
        else:
            x = x + jnp.where(pos >= s, pltpu.roll(x, s, axis=0), 0.0)
        s *= 2
    return x


def _scan_block(q, k, v, la, slow_ref, mask_ref, st_ref, u_ref, o_ref, *, c, reverse):
    tb = q.shape[0]
    n = tb // c
    hb, dv, dk = st_ref.shape
    chunks = [slice(ci * c, (ci + 1) * c) for ci in range(n)]
    kcols = [slice(j * dk, (j + 1) * dk) for j in range(hb)]
    vcols = [slice(j * dv, (j + 1) * dv) for j in range(hb)]
    ref_row = c // 2 if reverse else c // 2 - 1
    end_row = 0 if reverse else c - 1
    b = _chunk_cumsum(la, c, reverse)
    b_ref = jnp.concatenate([jnp.broadcast_to(b[sl][ref_row:ref_row + 1], (c, b.shape[1])) for sl in chunks], axis=0)
    b_end = jnp.concatenate([jnp.broadcast_to(b[sl][end_row:end_row + 1], (c, b.shape[1])) for sl in chunks], axis=0)
    zq = b - b_ref
    safe = jnp.max(jnp.abs(zq)) < SAFE_EXPONENT
    q_in = (q * jnp.exp(b)).astype(BF16)
    k_up = (k * jnp.exp(b_end - b)).astype(BF16)
    v_bf = v.astype(BF16)
    row = lax.broadcasted_iota(jnp.int32, (c, c), 0)
    col = lax.broadcasted_iota(jnp.int32, (c, c), 1)
    visible = (col >= row) if reverse else (col <= row)
    n_levels = slow_ref.shape[0] // c

    def fast_scores():
        q_mid = (q * jnp.exp(zq)).astype(BF16)
        k_mid = (k * jnp.exp(-zq)).astype(BF16)
        return jnp.concatenate([jnp.where(visible, _dot_nt(q_mid[sl, kc], k_mid[sl, kc]), 0.0)
                                for kc in kcols for sl in chunks], axis=0)

    def slow_scores():
        q_bf = q.astype(BF16)
        k_bf = k.astype(BF16)
        parts = _split3(la)
        out = []
        for kc in kcols:
            for sl in chunks:
                zs = _dot_exact_lhs(slow_ref[...], tuple(p[sl, kc] for p in parts))
                a = mask_ref[n_levels] * _dot_nt(q_bf[sl, kc], k_bf[sl, kc])
                for lv in range(n_levels):
                    e = jnp.exp(zs[lv * c:(lv + 1) * c])
                    a += mask_ref[lv] * _dot_nt((q[sl, kc] * e).astype(BF16), (k[sl, kc] * e).astype(BF16))
                out.append(a)
        return jnp.concatenate(out, axis=0)

    a = lax.cond(safe, fast_scores, slow_scores).astype(BF16)
    for j in range(hb):
        for ci, sl in enumerate(chunks):
            u_ref[j * n + ci] = _dot_tn(v_bf[sl, vcols[j]], k_up[sl, kcols[j]])
    entry = {}
    for j in range(hb):
        st = st_ref[j]
        for ci in (range(n - 1, -1, -1) if reverse else range(n)):
            entry[j, ci] = st.astype(BF16)
            st = st * jnp.exp(b_end[chunks[ci], kcols[j]][0:1]) + u_ref[j * n + ci]
        st_ref[j] = st
    for j in range(hb):
        for ci, sl in enumerate(chunks):
            rows = slice(j * tb + ci * c, j * tb + (ci + 1) * c)
            o_ref[sl, vcols[j]] = _dot_nt(q_in[sl, kcols[j]], entry[j, ci]) + _dot(a[rows], v_bf[sl, vcols[j]])


def _scan_finish(o_ref, prev_ref, gate_ref, gn_ref, y_ref):
    dv = gn_ref.shape[1]
    o = o_ref[...] + prev_ref[...]
    gate = _silu(gate_ref[...])
    for j in range(o.shape[1] // dv):
        cols = slice(j * dv, (j + 1) * dv)
        oj = o[:, cols]
        oj = oj * lax.rsqrt(jnp.mean(oj * oj, axis=-1, keepdims=True) + EPS) * gn_ref[...]
        y_ref[:, cols] = (oj * gate[:, cols]).astype(BF16)


def _scan_state_io(s0_ref, sT_ref, st_ref, do_body):
    t = pl.program_id(2)
    hb = st_ref.shape[0]

    @pl.when(t == 0)
    def _():
        for j in range(hb):
            st_ref[j] = s0_ref[j].T

    do_body()

    @pl.when(t == pl.num_programs(2) - 1)
    def _():
        for j in range(hb):
            sT_ref[j] = st_ref[j].T


def _without(refs, pos):
    return refs if pos is None else refs[:pos] + refs[pos + 1:]


def _hgrn_kernel(*refs, c, reverse, last_pass, carry_pos):
    refs = _without(refs, carry_pos)
    if last_pass:
        (q_ref, f_ref, i_ref, lb_ref, s0_ref, slow_ref, mask_ref, prev_ref, gate_ref, gn_ref,
         y_ref, sT_ref, st_ref, u_ref, o_scr) = refs
    else:
        (q_ref, f_ref, i_ref, lb_ref, s0_ref, slow_ref, mask_ref,
         y_ref, sT_ref, st_ref, u_ref) = refs
        o_scr = y_ref

    def body():
        lb = lb_ref[...]
        f = lb + (1.0 - lb) * _sigmoid(f_ref[...])
        _scan_block(_silu(q_ref[...]), 1.0 - f, i_ref[...], jnp.log(f),
                    slow_ref, mask_ref, st_ref, u_ref, o_scr, c=c, reverse=reverse)
        if last_pass:
            _scan_finish(o_scr, prev_ref, gate_ref, gn_ref, y_ref)

    _scan_state_io(s0_ref, sT_ref, st_ref, body)


def _gla_kernel(*refs, c, reverse, last_pass, carry_pos, rank, q_scale):
    refs = _without(refs, carry_pos)
    if last_pass:
        (q_ref, k_ref, v_ref, r_ref, wd_ref, bd_ref, s0_ref, slow_ref, mask_ref,
         prev_ref, gate_ref, gn_ref, y_ref, sT_ref, st_ref, u_ref, o_scr) = refs
    else:
        (q_ref, k_ref, v_ref, r_ref, wd_ref, bd_ref, s0_ref, slow_ref, mask_ref,
         y_ref, sT_ref, st_ref, u_ref) = refs
        o_scr = y_ref

    def body():
        lo = rank if reverse else 0
        r = r_ref[...][:, lo:lo + rank].astype(BF16)
        logit = _dot(r, wd_ref[...].astype(BF16)) + bd_ref[...]
        la = (jnp.minimum(logit, 0.0) - jnp.log(1.0 + jnp.exp(-jnp.abs(logit)))) / GLA_TAU
        _scan_block(q_ref[...] * q_scale, k_ref[...], v_ref[...], la,
                    slow_ref, mask_ref, st_ref, u_ref, o_scr, c=c, reverse=reverse)
        if last_pass:
            _scan_finish(o_scr, prev_ref, gate_ref, gn_ref, y_ref)

    _scan_state_io(s0_ref, sT_ref, st_ref, body)


def _scan_call(kernel, proj, col_specs, extra, s0, prev, gate_col, gn, nb, t, heads, dk, dv, row0,
               reverse, out_row0=0, out_rows=None, carry=None, **kw):
    tb = _pick_tile(t, (SCAN_BLOCK, 256, 128))
    c = min(SCAN_CHUNK, tb)
    nblk = t // tb
    rb0 = row0 // tb
    last_pass = prev is not None
    slow, masks = _scan_constants(c, reverse)

    def tblk(i):
        return (nblk - 1 - i) if reverse else i

    hb = _pick_tile(heads, (SCAN_HEADS, 2, 1))

    def col(off, width, per_head=True):
        if not per_head:
            return pl.BlockSpec((tb, width), lambda b, h, i: (rb0 + b * nblk + tblk(i), off // width))
        assert off % (hb * width) == 0
        return pl.BlockSpec((tb, hb * width),
                            lambda b, h, i: (rb0 + b * nblk + tblk(i), off // (hb * width) + h))

    in_specs = [col(*spec[1:]) for spec in col_specs]
    args = [spec[0] for spec in col_specs]
    for e in extra:
        in_specs.append(pl.BlockSpec((e.shape[0], hb * dk), lambda b, h, i: (0, h)))
        args.append(e)
    in_specs.append(pl.BlockSpec((None, hb, dk, dv), lambda b, h, i: (b, h, 0, 0)))
    args.append(s0)
    for cst in (slow, masks):
        in_specs.append(pl.BlockSpec(cst.shape, lambda b, h, i, nd=cst.ndim: (0,) * nd))
        args.append(cst)
    scratch = [pltpu.VMEM((hb, dv, dk), F32), pltpu.VMEM((hb * (tb // c), dv, dk), F32)]
    if last_pass:
        in_specs += [pl.BlockSpec((tb, hb * dv), lambda b, h, i: (b * nblk + tblk(i), h)),
                     col(gate_col, dv),
                     pl.BlockSpec((1, dv), lambda b, h, i: (0, 0))]
        args += [prev, proj, gn]
        scratch.append(pltpu.VMEM((tb, hb * dv), F32))
    y_dtype = BF16 if last_pass else F32
    orb = out_row0 // tb
    out_rows = nb * t if out_rows is None else out_rows
    aliases, carry_pos = {}, None
    if carry is not None:
        carry_pos = len(args)
        aliases = {carry_pos: 0}
        in_specs.append(pl.BlockSpec(memory_space=pl.ANY))
        args.append(carry)
    return pl.pallas_call(
        functools.partial(kernel, c=c, reverse=reverse, last_pass=last_pass, carry_pos=carry_pos, **kw),
        grid=(nb, heads // hb, nblk),
        in_specs=in_specs,
        out_specs=[pl.BlockSpec((tb, hb * dv), lambda b, h, i: (orb + b * nblk + tblk(i), h)),
                   pl.BlockSpec((None, hb, dk, dv), lambda b, h, i: (b, h, 0, 0))],
        out_shape=[jax.ShapeDtypeStruct((out_rows, heads * dv), y_dtype),
                   jax.ShapeDtypeStruct((nb, heads, dk, dv), F32)],
        input_output_aliases=aliases,
        scratch_shapes=scratch,
        compiler_params=_params(("parallel", "parallel", "arbitrary")),
    )(*args)


class _Geometry:
    pass


def _rope_tables(n_tok, da):
    ax = da // 2
    rows = n_tok // GRID_W
    r = jnp.broadcast_to(jnp.arange(rows, dtype=F32)[:, None], (rows, GRID_W)).reshape(-1)
    col = jnp.broadcast_to(jnp.arange(GRID_W, dtype=F32)[None, :], (rows, GRID_W)).reshape(-1)
    inv = ROPE_BASE ** (-jnp.arange(0, ax, 2, dtype=F32) / ax)
    ar, ac = r[:, None] * inv, col[:, None] * inv
    ang = jnp.concatenate([ar, ar, ac, ac], axis=-1)
    cos, sin = jnp.cos(ang), jnp.sin(ang)
    q = da // 4
    sign = jnp.where((jnp.arange(da) % (2 * q)) < q, -1.0, 1.0).astype(F32)
    return jnp.tile(cos, (1, 2)), jnp.tile(sin * sign, (1, 2))


def kernel(x_prompt, x_sample, c, cache_attn_k, cache_attn_v, state_hgrn, state_gla, c_ctx, ada_w, ada_b, norm_w, ffn_w_gate, ffn_w_up, ffn_w_down, w_in, attn_qk_norm, attn_lambda, attn_out_norm, hgrn_lb_logits, hgrn_out_norm, gla_w_decay, gla_b_decay, gla_out_norm, w_branch, w_out):
    batch, seq, d = x_prompt.shape
    dec_batch, dec_seq, _ = x_sample.shape
    depth = ada_w.shape[0]
    ha, da = cache_attn_k.shape[2], cache_attn_k.shape[5]
    dva = 2 * da
    hh, hdk, hdv = state_hgrn.shape[3:]
    hg, gdk, gdv = state_gla.shape[3:]
    rank = gla_w_decay.shape[2]
    sizes = (ha * 2 * da, ha * 2 * da, ha * dva, hh * hdk, hh * hdk, hh * hdk, hh * hdv, hh * hdv,
             hg * gdk, hg * gdk, hg * gdv, hg * gdv, 2 * rank, 3 * d)
    assert sum(sizes) == w_in.shape[2] and hdk == LANES and hdv == LANES and gdk == LANES and dva == LANES

    geo = _Geometry()
    geo.n_ctx_rows = batch * seq
    geo.dec_seq = dec_seq
    geo.tile_rows = math.gcd(geo.n_ctx_rows, dec_seq)
    n_ctx = geo.n_ctx_rows

    src = [0] + [int(v) for v in np.cumsum(sizes)]
    n_main = src[12]
    names = ('aq', 'ak', 'av', 'hq', 'hf_f', 'hf_b', 'hi', 'hg', 'gq', 'gk', 'gv', 'gg')
    off = {nm: src[i] for i, nm in enumerate(names)}

    n_cond = 1 + dec_batch
    cond = jnp.concatenate([c_ctx[None], c, jnp.zeros((-n_cond % 8, d), F32)], axis=0)
    mod = _adaln(cond, ada_w, ada_b).reshape(depth, cond.shape[0], 9, d)

    lb_w = jax.nn.softmax(hgrn_lb_logits.astype(F32), axis=0)
    lower_bounds = jnp.cumsum(lb_w, axis=0) - lb_w[0]
    rope_tabs = _rope_tables(dec_seq, da)

    n_lat = dec_batch * dec_seq
    nt = n_ctx + n_lat
    zeros_h = jnp.zeros((batch, hh, hdk, hdv), F32)
    zeros_g = jnp.zeros((batch, hg, gdk, gdv), F32)
    ks_l, vs_l, sh_l, sg_l = [], [], [], []
    x = None
    wg_bf, wu_bf, wd_bf = ffn_w_gate.astype(BF16), ffn_w_up.astype(BF16), ffn_w_down.astype(BF16)
    wb_bf, wo_bf = w_branch.astype(BF16), w_out.astype(BF16)
    cache_k = cache_attn_k.reshape(cache_attn_k.shape[:4] + (2 * da,))
    br_a = br_h = br_g = None

    for l in range(depth):
        m_l = mod[l]
        nw = norm_w[l]
        ffn1 = (m_l, nw[0:1], wg_bf, wu_bf, wd_bf, (l, 0), 0, geo)
        if l == 0:
            x = jnp.concatenate([x_prompt.reshape(n_ctx, d), x_sample.reshape(n_lat, d)], axis=0)
            br_a = jnp.zeros((nt, ha * dva), BF16)
            br_h = jnp.zeros((nt, hh * hdv), BF16)
            br_g = jnp.zeros((nt, hg * gdv), BF16)
        x, h_mix = _ffn(x, *ffn1, nxt=(nw[1:2], 3))
        w_l = w_in[l]
        proj = _matmul(h_mix, w_l[:, :n_main].astype(BF16), F32)
        w_glr = jnp.pad(w_l[:, src[12]:src[13]], ((0, 0), (0, LANES - 2 * rank))).astype(BF16)
        glr = _matmul(h_mix, w_glr, F32)
        gates = _matmul(h_mix, w_l[:, src[13]:].astype(BF16), BF16, gate=True)

        lam_init = 0.8 - 0.6 * math.exp(-0.3 * l)
        gains = jnp.tile(attn_qk_norm[l], (1, 2))
        cols = (off['aq'], off['ak'], off['av'])
        gn_a = attn_out_norm[l][:, None]
        qp, kp, vp, k32, v32 = _attn_prep(proj, cols, gains, None, batch, seq, ha, da, 0, True)
        ks_l.append(k32.reshape(batch, ha, seq, 2, da))
        vs_l.append(v32)
        qs, ksm, vsm = _attn_prep(proj, cols, gains, rope_tabs, dec_batch, dec_seq, ha, da, n_ctx, False)
        attn = (attn_lambda[l], gn_a, lam_init, da)
        k_all = jnp.concatenate([cache_k[:, l].astype(BF16), ksm], axis=2)
        v_all = jnp.concatenate([jnp.swapaxes(cache_attn_v[:, l], 2, 3).astype(BF16), vsm], axis=3)
        br_a = _attention(qp, kp, vp, *attn, out_row0=0, carry=br_a)
        br_a = _attention(qs, k_all, v_all, *attn, out_row0=n_ctx, carry=br_a)

        lb = lower_bounds[l][:, None]
        gn_h = hgrn_out_norm[l][None]

        def hgrn(nb, t, row0, s0, carry):
            specs_b = ((proj, off['hq'], hdk), (proj, off['hf_b'], hdk), (proj, off['hi'], hdv))
            specs_f = ((proj, off['hq'], hdk), (proj, off['hf_f'], hdk), (proj, off['hi'], hdv))
            o_b, s_b = _scan_call(_hgrn_kernel, proj, specs_b, [lb[1]], s0[1], None, None, None,
                                  nb, t, hh, hdk, hdv, row0, True)
            y, s_f = _scan_call(_hgrn_kernel, proj, specs_f, [lb[0]], s0[0], o_b, off['hg'], gn_h,
                                nb, t, hh, hdk, hdv, row0, False, out_row0=row0, out_rows=nt, carry=carry)
            return y, jnp.stack([s_f, s_b], axis=1)

        br_h, s_h = hgrn(batch, seq, 0, (zeros_h, zeros_h), br_h)
        br_h, _ = hgrn(dec_batch, dec_seq, n_ctx, (state_hgrn[:, l, 0], state_hgrn[:, l, 1]), br_h)
        sh_l.append(s_h)

        wd = gla_w_decay[l]
        bd = gla_b_decay[l][:, None]
        gn_g = gla_out_norm[l][None]

        def gla(nb, t, row0, s0, carry):
            specs = ((proj, off['gq'], gdk), (proj, off['gk'], gdk), (proj, off['gv'], gdv),
                     (glr, 0, LANES, False))
            kw = dict(rank=rank, q_scale=gdk ** -0.5)
            o_b, s_b = _scan_call(_gla_kernel, proj, specs, [wd[1], bd[1]], s0[1], None, None, None,
                                  nb, t, hg, gdk, gdv, row0, True, **kw)
            y, s_f = _scan_call(_gla_kernel, proj, specs, [wd[0], bd[0]], s0[0], o_b, off['gg'], gn_g,
                                nb, t, hg, gdk, gdv, row0, False, out_row0=row0, out_rows=nt, carry=carry,
                                **kw)
            return y, jnp.stack([s_f, s_b], axis=1)

        br_g, s_g = gla(batch, seq, 0, (zeros_g, zeros_g), br_g)
        br_g, _ = gla(dec_batch, dec_seq, n_ctx, (state_gla[:, l, 0], state_gla[:, l, 1]), br_g)
        sg_l.append(s_g)

        x = _mix_out(br_a, br_h, br_g, gates, wb_bf, wo_bf, l, x, m_l, 5, geo)
        ffn2 = (m_l, nw[2:3], wg_bf, wu_bf, wd_bf, (l, 1), 6, geo)
        if l + 1 < depth:
            x = _ffn(x, *ffn2)[0]
        else:
            y_prompt = _ffn(x, *ffn2, rows=n_ctx)[0].reshape(batch, seq, d)
            y_sample = _ffn(x, *ffn2, rows=n_lat, in_row0=n_ctx)[0].reshape(dec_batch, dec_seq, d)

    return (y_prompt, y_sample, jnp.stack(ks_l, axis=1), jnp.stack(vs_l, axis=1),
            jnp.stack(sh_l, axis=1), jnp.stack(sg_l, axis=1))
```
